```python
import math
import jax, jax.numpy as jnp
from jax import lax
import numpy as np

D_MODEL = 2048
BATCH = 2
SEQ = 4096
DEPTH = 1

D_MIX = D_MODEL
D_ATTN = D_MIX // 2
D_CONV = D_MIX - D_ATTN
N_HEADS = 8
DV = D_ATTN // N_HEADS
DQ = DV // 2
ROT_DIM = DQ // 4
ROPE_THETA = 500000.0
CONV_WIDTH = 31
CONV_GROUPS = 8
Q_BLOCK = 128
EPS = 1e-6
D_IN = 3 * D_ATTN + D_ATTN + 2 * D_CONV + D_CONV

kernel_name = "hymba_diffattn_conformer_conv_layer"


def rmsnorm(x, g):
    xf = x.astype(jnp.float32)
    y = xf * lax.rsqrt(jnp.mean(xf * xf, axis=-1, keepdims=True) + EPS)
    return (y * g.astype(jnp.float32)).astype(x.dtype)


def layernorm(x, g, b):
    xf = x.astype(jnp.float32)
    mu = jnp.mean(xf, axis=-1, keepdims=True)
    var = jnp.mean(jnp.square(xf - mu), axis=-1, keepdims=True)
    y = (xf - mu) * lax.rsqrt(var + EPS)
    return (y * g.astype(jnp.float32) + b.astype(jnp.float32)).astype(x.dtype)


def rope_tables(positions):
    inv_freq = ROPE_THETA ** (-jnp.arange(0, ROT_DIM, 2, dtype=jnp.float32) / ROT_DIM)
    ang = positions.astype(jnp.float32)[..., None] * inv_freq
    return jnp.cos(ang), jnp.sin(ang)


def apply_partial_rope(t, cos, sin):
    cos = cos[:, :, None, None, :].astype(t.dtype)
    sin = sin[:, :, None, None, :].astype(t.dtype)
    t_rot, t_pass = t[..., :ROT_DIM], t[..., ROT_DIM:]
    t1, t2 = t_rot[..., : ROT_DIM // 2], t_rot[..., ROT_DIM // 2:]
    rotated = jnp.concatenate([t1 * cos - t2 * sin, t2 * cos + t1 * sin], axis=-1)
    return jnp.concatenate([rotated, t_pass], axis=-1)


def diff_attention(q, k, v, lam):
    B, H, S = q.shape[0], q.shape[1], q.shape[2]
    scale = 1.0 / math.sqrt(DQ)
    kpos = jnp.arange(S)

    def one_block(i):
        start = i * Q_BLOCK
        qb = lax.dynamic_slice_in_dim(q, start, Q_BLOCK, axis=2)
        s = jnp.einsum('bhqcd,bhkcd->bhcqk', qb, k).astype(jnp.float32) * scale
        qpos = start + jnp.arange(Q_BLOCK)
        causal = kpos[None, :] <= qpos[:, None]
        s = jnp.where(causal, s, -1e30)
        p = jax.nn.softmax(s, axis=-1)
        a = p[:, :, 0] - lam * p[:, :, 1]
        return jnp.einsum('bhqk,bhkd->bhqd', a.astype(v.dtype), v)

    outs = lax.map(one_block, jnp.arange(S // Q_BLOCK))
    outs = jnp.transpose(outs, (1, 0, 3, 2, 4))
    return outs.reshape(B, S, H, DV)


def causal_depthwise_conv(y, w, b):
    out = lax.conv_general_dilated(
        y, w[:, None, :].astype(y.dtype), window_strides=(1,),
        padding=[(CONV_WIDTH - 1, 0)],
        dimension_numbers=('NWC', 'WIO', 'NWC'),
        feature_group_count=y.shape[-1])
    return out + b.astype(y.dtype)


def setup_inputs(seed: int = 0) -> dict:
    key = jax.random.key(seed)
    ks = jax.random.split(key, 24)
    f32 = jnp.float32
    nrm = lambda k, shape, s: jax.random.normal(k, shape, f32) * s
    return {
        "x": nrm(ks[0], (BATCH, SEQ, D_MODEL), 1.0),
        "c": nrm(ks[1], (BATCH, D_MODEL), 1.0),
        "positions": jnp.broadcast_to(jnp.arange(SEQ, dtype=jnp.int32), (BATCH, SEQ)),
        "norm_g": 1.0 + nrm(ks[2], (DEPTH, D_MODEL), 0.01),
        "w_ada": nrm(ks[3], (DEPTH, D_MODEL, 3 * D_MODEL), D_MODEL ** -0.5),
        "b_ada": nrm(ks[4], (DEPTH, 3 * D_MODEL), 0.01),
        "w_in": nrm(ks[5], (DEPTH, D_MODEL, D_IN), D_MODEL ** -0.5),
        "lambda_q1": nrm(ks[6], (DEPTH, DQ), 0.1),
        "lambda_k1": nrm(ks[7], (DEPTH, DQ), 0.1),
        "lambda_q2": nrm(ks[8], (DEPTH, DQ), 0.1),
        "lambda_k2": nrm(ks[9], (DEPTH, DQ), 0.1),
        "subln_g": 1.0 + nrm(ks[10], (DEPTH, DV), 0.01),
        "conv_dw_w": nrm(ks[11], (DEPTH, CONV_WIDTH, D_CONV), CONV_WIDTH ** -0.5),
        "conv_dw_b": nrm(ks[12], (DEPTH, D_CONV), 0.01),
        "conv_ln_g": 1.0 + nrm(ks[13], (DEPTH, D_CONV), 0.01),
        "conv_ln_b": nrm(ks[14], (DEPTH, D_CONV), 0.01),
        "w_pw": nrm(ks[15], (DEPTH, D_CONV, D_CONV), D_CONV ** -0.5),
        "b_pw": nrm(ks[16], (DEPTH, D_CONV), 0.01),
        "w_out": nrm(ks[17], (DEPTH, D_MIX, D_MODEL), D_MIX ** -0.5),
        "final_g": 1.0 + nrm(ks[18], (D_MODEL,), 0.01),
    }


def reference(x, c, positions, norm_g, w_ada, b_ada, w_in, lambda_q1, lambda_k1,
              lambda_q2, lambda_k2, subln_g, conv_dw_w, conv_dw_b, conv_ln_g, conv_ln_b,
              w_pw, b_pw, w_out, final_g):
    B, S, _ = x.shape
    cos, sin = rope_tables(positions)
    c_act = jax.nn.silu(c)
    splits = [D_ATTN, 2 * D_ATTN, 3 * D_ATTN, 4 * D_ATTN, 4 * D_ATTN + 2 * D_CONV]

    for l in range(DEPTH):
        mod = c_act @ w_ada[l] + b_ada[l]
        shift, scale, gate = jnp.split(mod, 3, axis=-1)
        h = rmsnorm(x, norm_g[l]) * (1.0 + scale[:, None, :]) + shift[:, None, :]

        z = h @ w_in[l]
        q, k, v, g_attn, u, g_conv = jnp.split(z, splits, axis=-1)

        lam_init = 0.8 - 0.6 * math.exp(-0.3 * l)
        lam = (jnp.exp(jnp.sum(lambda_q1[l].astype(jnp.float32) * lambda_k1[l].astype(jnp.float32)))
               - jnp.exp(jnp.sum(lambda_q2[l].astype(jnp.float32) * lambda_k2[l].astype(jnp.float32)))
               + lam_init)
        q = apply_partial_rope(q.reshape(B, S, N_HEADS, 2, DQ), cos, sin)
        k = apply_partial_rope(k.reshape(B, S, N_HEADS, 2, DQ), cos, sin)
        q = jnp.transpose(q, (0, 2, 1, 3, 4))
        k = jnp.transpose(k, (0, 2, 1, 3, 4))
        vh = jnp.transpose(v.reshape(B, S, N_HEADS, DV), (0, 2, 1, 3))
        o = diff_attention(q, k, vh, lam)
        o = rmsnorm(o, subln_g[l]) * (1.0 - lam_init)
        y_attn = o.reshape(B, S, D_ATTN) * jax.nn.silu(g_attn)

        u_a, u_b = jnp.split(u, 2, axis=-1)
        y = u_a * jax.nn.sigmoid(u_b)
        y = causal_depthwise_conv(y, conv_dw_w[l], conv_dw_b[l])
        y = jax.nn.silu(layernorm(y, conv_ln_g[l], conv_ln_b[l]))
        y = y @ w_pw[l] + b_pw[l]
        y_conv = y * jax.nn.silu(g_conv)

        mixed = jnp.concatenate([y_attn, y_conv], axis=-1) @ w_out[l]
        x = x + gate[:, None, :] * mixed

    return rmsnorm(x, final_g)
```

```python
import functools
import math

import jax
import jax.numpy as jnp
from jax import lax
from jax.experimental import pallas as pl
from jax.experimental.pallas import tpu as pltpu

F32 = jnp.float32
BF16 = jnp.bfloat16

N_HEADS = 8
CONV_WIDTH = 31
ROPE_THETA = 500000.0
EPS = 1e-6
LANES = 128
SUBLANES = 8
HALO = 32
NEG_BIG = -1e30
VMEM_LIMIT = 56 * 1024 * 1024


def _silu(t):
    return t * jax.nn.sigmoid(t)


def _ada_kernel(c_ref, w_ref, b_ref, o_ref):
    ca = _silu(c_ref[...])
    o_ref[...] = jnp.dot(ca.astype(BF16), w_ref[...].astype(BF16),
                         preferred_element_type=F32) + b_ref[...]


def _ada_mod(c_pad, w_ada, b_ada, tn=1024):
    rows, d = c_pad.shape
    n = w_ada.shape[1]
    return pl.pallas_call(
        _ada_kernel,
        grid=(n // tn,),
        in_specs=[pl.BlockSpec((rows, d), lambda j: (0, 0)),
                  pl.BlockSpec((d, tn), lambda j: (0, j)),
                  pl.BlockSpec((1, tn), lambda j: (0, j))],
        out_specs=pl.BlockSpec((rows, tn), lambda j: (0, j)),
        out_shape=jax.ShapeDtypeStruct((rows, n), F32),
        compiler_params=pltpu.CompilerParams(
            dimension_semantics=("arbitrary",), vmem_limit_bytes=VMEM_LIMIT),
        name="ada_mod",
    )(c_pad, w_ada, b_ada)


def _inproj_kernel(x_ref, g_ref, scale_ref, shift_ref, pos_ref, freq_ref, w_ref, o_ref,
                   h_ref, cos_ref, sina_ref, sinb_ref, *, dq, rot, q_scale):
    j = pl.program_id(1)

    @pl.when(j == 0)
    def _():
        x = x_ref[...]
        ms = jnp.mean(x * x, axis=-1, keepdims=True)
        y = x * lax.rsqrt(ms + EPS) * g_ref[...]
        h_ref[...] = (y * (1.0 + scale_ref[0]) + shift_ref[0]).astype(BF16)
        ang = pos_ref[...].astype(F32) * freq_ref[...]
        d = lax.broadcasted_iota(jnp.int32, ang.shape, 1) % dq
        cs, sn = jnp.cos(ang), jnp.sin(ang)
        cos_ref[...] = jnp.where(d < rot, cs, 1.0)
        sina_ref[...] = jnp.where(d < rot // 2, -sn, 0.0)
        sinb_ref[...] = jnp.where((d >= rot // 2) & (d < rot), sn, 0.0)

    acc = jnp.dot(h_ref[...], w_ref[...], preferred_element_type=F32)

    @pl.when(j < 2)
    def _():
        sc = jnp.where(j == 0, q_scale, 1.0).astype(F32)
        cs, sa, sb = cos_ref[...] * sc, sina_ref[...] * sc, sinb_ref[...] * sc
        for t in range(acc.shape[1] // LANES):
            a = acc[:, t * LANES:(t + 1) * LANES]
            up = pltpu.roll(a, LANES - rot // 2, 1)
            dn = pltpu.roll(a, rot // 2, 1)
            o_ref[:, t * LANES:(t + 1) * LANES] = (a * cs + up * sa + dn * sb).astype(o_ref.dtype)

    @pl.when(j >= 2)
    def _():
        o_ref[...] = acc.astype(o_ref.dtype)


def _in_proj(x2, norm_g, mod3, pos2, freq_lane, w_in_bf, *, seq, dq, rot, tm=1024, tn=1024):
    m, d = x2.shape
    n = w_in_bf.shape[1]
    blocks_per_seq = seq // tm
    kern = functools.partial(_inproj_kernel, dq=dq, rot=rot, q_scale=1.0 / math.sqrt(dq))
    return pl.pallas_call(
        kern,
        grid=(m // tm, n // tn),
        in_specs=[
            pl.BlockSpec((tm, d), lambda i, j: (i, 0)),
            pl.BlockSpec((1, d), lambda i, j: (0, 0)),
            pl.BlockSpec((1, 1, d), lambda i, j: ((i // blocks_per_seq) * 3 + 1, 0, 0)),
            pl.BlockSpec((1, 1, d), lambda i, j: ((i // blocks_per_seq) * 3 + 0, 0, 0)),
            pl.BlockSpec((tm, 1), lambda i, j: (i, 0)),
            pl.BlockSpec((1, LANES), lambda i, j: (0, 0)),
            pl.BlockSpec((d, tn), lambda i, j: (0, j)),
        ],
        out_specs=pl.BlockSpec((tm, tn), lambda i, j: (i, j)),
        out_shape=jax.ShapeDtypeStruct((m, n), BF16),
        scratch_shapes=[pltpu.VMEM((tm, d), BF16),
                        pltpu.VMEM((tm, LANES), F32),
                        pltpu.VMEM((tm, LANES), F32),
                        pltpu.VMEM((tm, LANES), F32)],
        compiler_params=pltpu.CompilerParams(
            dimension_semantics=("arbitrary", "arbitrary"), vmem_limit_bytes=VMEM_LIMIT),
        name="in_proj",
    )(x2, norm_g, mod3, mod3, pos2, freq_lane, w_in_bf)


def _attn_kernel(lam_ref, sg_ref, q_ref, k_ref, v_ref, g_ref, o_ref,
                 vt_ref, m_ref, l_ref, acc_ref, *, qb, kb, dq, lam_init):
    i = pl.program_id(2)
    dv = v_ref.shape[1]

    @pl.when(i == 0)
    def _():
        vt_ref[...] = v_ref[...].astype(F32).T.astype(BF16)

    q = q_ref[...]
    lane = lax.broadcasted_iota(jnp.int32, q.shape, 1)
    zero = jnp.zeros_like(q)
    qq = jnp.concatenate([jnp.where(lane < dq, q, zero), jnp.where(lane >= dq, q, zero)], axis=0)

    m_ref[...] = jnp.full(m_ref.shape, NEG_BIG, F32)
    l_ref[...] = jnp.zeros(l_ref.shape, F32)
    acc_ref[...] = jnp.zeros(acc_ref.shape, F32)

    def step(j, masked):
        k0 = pl.multiple_of(j * kb, kb)
        kblk = k_ref[pl.ds(k0, kb), :]
        s = lax.dot_general(kblk, qq, (((1,), (1,)), ((), ())), preferred_element_type=F32)
        if masked:
            kpos = k0 + lax.broadcasted_iota(jnp.int32, s.shape, 0)
            qpos = i * qb + lax.broadcasted_iota(jnp.int32, s.shape, 1) % qb
            s = jnp.where(kpos <= qpos, s, NEG_BIG)
        m_old = m_ref[...]
        m_new = jnp.maximum(m_old, jnp.max(s, axis=0, keepdims=True))
        alpha = jnp.exp(m_old - m_new)
        p = jnp.exp(s - m_new)
        l_ref[...] = alpha * l_ref[...] + jnp.sum(p, axis=0, keepdims=True)
        m_ref[...] = m_new
        pv = jnp.dot(vt_ref[:, pl.ds(k0, kb)], p.astype(BF16), preferred_element_type=F32)
        acc_ref[...] = alpha * acc_ref[...] + pv

    def body(j, carry):
        step(j, masked=False)
        return carry

    lax.fori_loop(0, i, body, 0)
    step(i, masked=True)

    lam = (jnp.exp(jnp.sum(lam_ref[0:1, :] * lam_ref[1:2, :], axis=-1, keepdims=True))
           - jnp.exp(jnp.sum(lam_ref[2:3, :] * lam_ref[3:4, :], axis=-1, keepdims=True))
           + lam_init)
    o_all = acc_ref[...] / l_ref[...]
    o = o_all[:, :qb] - lam * o_all[:, qb:]
    ms = jnp.mean(o * o, axis=0, keepdims=True)
    on = (o * lax.rsqrt(ms + EPS)).T
    on = on * sg_ref[...] * (1.0 - lam_init)
    o_ref[...] = (on * _silu(g_ref[...].astype(F32))).astype(o_ref.dtype)


def _diff_attn(z, lam_params, subln_g, *, batch, seq, dq, dv, lam_init, qb=512):
    kb = qb
    nq = seq // qb
    h = N_HEADS
    k_col0 = h * 2 * dq // LANES
    v_col0 = 2 * k_col0
    g_col0 = v_col0 + h * dv // LANES
    kern = functools.partial(_attn_kernel, qb=qb, kb=kb, dq=dq, lam_init=lam_init)
    return pl.pallas_call(
        kern,
        grid=(batch, h, nq),
        in_specs=[
            pl.BlockSpec(lam_params.shape, lambda b, hh, i: (0, 0)),
            pl.BlockSpec((1, dv), lambda b, hh, i: (0, 0)),
            pl.BlockSpec((qb, 2 * dq), lambda b, hh, i: (b * nq + i, hh)),
            pl.BlockSpec((seq, 2 * dq), lambda b, hh, i: (b, k_col0 + hh)),
            pl.BlockSpec((seq, dv), lambda b, hh, i: (b, v_col0 + hh)),
            pl.BlockSpec((qb, dv), lambda b, hh, i: (b * nq + i, g_col0 + hh)),
        ],
        out_specs=pl.BlockSpec((qb, dv), lambda b, hh, i: (b * nq + i, hh)),
        out_shape=jax.ShapeDtypeStruct((batch * seq, h * dv), BF16),
        scratch_shapes=[pltpu.VMEM((dv, seq), BF16),
                        pltpu.VMEM((1, 2 * qb), F32),
                        pltpu.VMEM((1, 2 * qb), F32),
                        pltpu.VMEM((dv, 2 * qb), F32)],
        compiler_params=pltpu.CompilerParams(
            dimension_semantics=("arbitrary", "arbitrary", "arbitrary"),
            vmem_limit_bytes=VMEM_LIMIT),
        name="diff_attn",
    )(lam_params, subln_g, z, z, z, z)


def _conv_kernel(ua_ref, ub_ref, ha_ref, hb_ref, gc_ref, dww_ref, dwb_ref, lng_ref, lnb_ref,
                 wpw_ref, bpw_ref, o_ref, sh_ref, conv_ref, *, ts, row_chunk, lane_chunk):
    i = pl.program_id(1)
    c = ua_ref.shape[1]

    def glu(a_ref, b_ref):
        return a_ref[...].astype(F32) * jax.nn.sigmoid(b_ref[...].astype(F32))

    hist = jnp.where(i > 0, glu(ha_ref, hb_ref), 0.0)
    ypad = jnp.concatenate([hist, glu(ua_ref, ub_ref)], axis=0)
    for r in range(SUBLANES):
        n_r = HALO + ts - (SUBLANES if r else 0)
        sh_ref[r, 0:n_r, :] = ypad[r:r + n_r, :]

    base = HALO - (CONV_WIDTH - 1)
    n_rc = ts // row_chunk
    for lc in range(c // lane_chunk):
        l0 = lc * lane_chunk

        def body(rc, carry, l0=l0):
            r0 = pl.multiple_of(rc * row_chunk, row_chunk)
            acc = jnp.zeros((row_chunk, lane_chunk), F32)
            for j in range(CONV_WIDTH):
                off = base + j
                a, r = off // SUBLANES, off % SUBLANES
                yv = sh_ref[r, pl.ds(r0 + a * SUBLANES, row_chunk), l0:l0 + lane_chunk]
                acc = acc + yv * dww_ref[j:j + 1, l0:l0 + lane_chunk]
            conv_ref[pl.ds(r0, row_chunk), l0:l0 + lane_chunk] = acc
            return carry

        lax.fori_loop(0, n_rc, body, 0)

    y = conv_ref[...] + dwb_ref[...]
    mu = jnp.mean(y, axis=-1, keepdims=True)
    var = jnp.mean(jnp.square(y - mu), axis=-1, keepdims=True)
    y = (y - mu) * lax.rsqrt(var + EPS) * lng_ref[...] + lnb_ref[...]
    y = _silu(y)
    y = jnp.dot(y.astype(BF16), wpw_ref[...], preferred_element_type=F32) + bpw_ref[...]
    o_ref[...] = (y * _silu(gc_ref[...].astype(F32))).astype(o_ref.dtype)


def _conv_mod(z, dw_w, dw_b, ln_g, ln_b, w_pw_bf, b_pw, *, batch, seq, c, ua_col0, ts=256):
    nt = seq // ts
    cb = c // c
    ua_blk, ub_blk, gc_blk = ua_col0 // c, ua_col0 // c + 1, ua_col0 // c + 2
    halo_per_ts = ts // HALO
    kern = functools.partial(_conv_kernel, ts=ts, row_chunk=32, lane_chunk=256)
    del cb

    def halo_map(blk):
        return lambda b, i: (jnp.maximum((b * nt + i) * halo_per_ts - 1, 0), blk)

    row = lambda b, i: (0, 0)
    return pl.pallas_call(
        kern,
        grid=(batch, nt),
        in_specs=[
            pl.BlockSpec((ts, c), lambda b, i: (b * nt + i, ua_blk)),
            pl.BlockSpec((ts, c), lambda b, i: (b * nt + i, ub_blk)),
            pl.BlockSpec((HALO, c), halo_map(ua_blk)),
            pl.BlockSpec((HALO, c), halo_map(ub_blk)),
            pl.BlockSpec((ts, c), lambda b, i: (b * nt + i, gc_blk)),
            pl.BlockSpec((CONV_WIDTH, c), row),
            pl.BlockSpec((1, c), row),
            pl.BlockSpec((1, c), row),
            pl.BlockSpec((1, c), row),
            pl.BlockSpec((c, c), row),
            pl.BlockSpec((1, c), row),
        ],
        out_specs=pl.BlockSpec((ts, c), lambda b, i: (b * nt + i, 0)),
        out_shape=jax.ShapeDtypeStruct((batch * seq, c), BF16),
        scratch_shapes=[pltpu.VMEM((SUBLANES, HALO + ts, c), F32),
                        pltpu.VMEM((ts, c), F32)],
        compiler_params=pltpu.CompilerParams(
            dimension_semantics=("arbitrary", "arbitrary"), vmem_limit_bytes=VMEM_LIMIT),
        name="conv_mod",
    )(z, z, z, z, z, dw_w, dw_b, ln_g, ln_b, w_pw_bf, b_pw)


def _outproj_kernel(ya_ref, yc_ref, wa_ref, wc_ref, x_ref, gate_ref, fg_ref, o_ref, *, final_norm):
    mixed = jnp.dot(ya_ref[...], wa_ref[...], preferred_element_type=F32)
    mixed = mixed + jnp.dot(yc_ref[...], wc_ref[...], preferred_element_type=F32)
    xn = x_ref[...] + gate_ref[0] * mixed
    if final_norm:
        ms = jnp.mean(xn * xn, axis=-1, keepdims=True)
        xn = xn * lax.rsqrt(ms + EPS) * fg_ref[...]
    o_ref[...] = xn


def _out_proj(y_attn, y_conv, w_out_bf, x2, mod3, final_g, *, seq, final_norm, tm=512):
    m, d = x2.shape
    da, dc = y_attn.shape[1], y_conv.shape[1]
    blocks_per_seq = seq // tm
    return pl.pallas_call(
        functools.partial(_outproj_kernel, final_norm=final_norm),
        grid=(m // tm,),
        in_specs=[
            pl.BlockSpec((tm, da), lambda i: (i, 0)),
            pl.BlockSpec((tm, dc), lambda i: (i, 0)),
            pl.BlockSpec((da, d), lambda i: (0, 0)),
            pl.BlockSpec((dc, d), lambda i: (da // dc, 0)),
            pl.BlockSpec((tm, d), lambda i: (i, 0)),
            pl.BlockSpec((1, 1, d), lambda i: ((i // blocks_per_seq) * 3 + 2, 0, 0)),
            pl.BlockSpec((1, d), lambda i: (0, 0)),
        ],
        out_specs=pl.BlockSpec((tm, d), lambda i: (i, 0)),
        out_shape=jax.ShapeDtypeStruct((m, d), F32),
        compiler_params=pltpu.CompilerParams(
            dimension_semantics=("arbitrary",), vmem_limit_bytes=VMEM_LIMIT),
        name="out_proj",
    )(y_attn, y_conv, w_out_bf, w_out_bf, x2, mod3, final_g)


def kernel(x, c, positions, norm_g, w_ada, b_ada, w_in, lambda_q1, lambda_k1, lambda_q2, lambda_k2,
           subln_g, conv_dw_w, conv_dw_b, conv_ln_g, conv_ln_b, w_pw, b_pw, w_out, final_g):
    batch, seq, d = x.shape
    depth = w_in.shape[0]
    d_attn = d // 2
    d_conv = d - d_attn
    dv = d_attn // N_HEADS
    dq = dv // 2
    rot = dq // 4

    inv_freq = ROPE_THETA ** (-jnp.arange(0, rot, 2, dtype=F32) / rot)
    freq_lane = jnp.tile(inv_freq, LANES // (rot // 2)).reshape(1, LANES)
    pos2 = positions.reshape(batch * seq, 1)
    c_pad = jnp.zeros((SUBLANES, d), F32).at[:batch].set(c)

    x2 = x.reshape(batch * seq, d)
    for l in range(depth):
        lam_init = 0.8 - 0.6 * math.exp(-0.3 * l)
        mod = _ada_mod(c_pad, w_ada[l], b_ada[l].reshape(1, -1))
        mod3 = mod[:batch].reshape(batch * 3, 1, d)
        z = _in_proj(x2, norm_g[l].reshape(1, d), mod3, pos2, freq_lane, w_in[l].astype(BF16),
                     seq=seq, dq=dq, rot=rot)
        lam_params = jnp.stack([lambda_q1[l], lambda_k1[l], lambda_q2[l], lambda_k2[l]]).astype(F32)
        y_attn = _diff_attn(z, lam_params, subln_g[l].reshape(1, dv),
                            batch=batch, seq=seq, dq=dq, dv=dv, lam_init=lam_init)
        y_conv = _conv_mod(z, conv_dw_w[l], conv_dw_b[l].reshape(1, -1), conv_ln_g[l].reshape(1, -1),
                           conv_ln_b[l].reshape(1, -1), w_pw[l].astype(BF16), b_pw[l].reshape(1, -1),
                           batch=batch, seq=seq, c=d_conv, ua_col0=4 * d_attn)
        x2 = _out_proj(y_attn, y_conv, w_out[l].astype(BF16), x2, mod3, final_g.reshape(1, d),
                       seq=seq, final_norm=(l == depth - 1))
    return x2.reshape(batch, seq, d)
```

```python
import functools
import math

import jax
import jax.numpy as jnp
from jax import lax
from jax.experimental import pallas as pl
from jax.experimental.pallas import tpu as pltpu

F32 = jnp.float32
BF16 = jnp.bfloat16

N_HEADS = 8
CONV_WIDTH = 31
ROPE_THETA = 500000.0
EPS = 1e-6
LANES = 128
SUBLANES = 8
HALO = 32
NEG_BIG = -1e30
ONES_ROWS = 16
LOG2E = math.log2(math.e)
VMEM_LIMIT = 56 * 1024 * 1024


def _silu(t):
    return t * jax.nn.sigmoid(t)


def _ada_kernel(c_ref, w_ref, b_ref, o_ref):
    ca = _silu(c_ref[...])
    o_ref[...] = jnp.dot(ca.astype(BF16), w_ref[...].astype(BF16),
                         preferred_element_type=F32) + b_ref[...]


def _ada_mod(c_pad, w_ada, b_ada, tn=1024):
    rows, d = c_pad.shape
    n = w_ada.shape[1]
    return pl.pallas_call(
        _ada_kernel,
        grid=(n // tn,),
        in_specs=[pl.BlockSpec((rows, d), lambda j: (0, 0)),
                  pl.BlockSpec((d, tn), lambda j: (0, j)),
                  pl.BlockSpec((1, tn), lambda j: (0, j))],
        out_specs=pl.BlockSpec((rows, tn), lambda j: (0, j)),
        out_shape=jax.ShapeDtypeStruct((rows, n), F32),
        compiler_params=pltpu.CompilerParams(
            dimension_semantics=("arbitrary",), vmem_limit_bytes=VMEM_LIMIT),
        name="ada_mod",
    )(c_pad, w_ada, b_ada)


def _inproj_kernel(x_ref, g_ref, scale_ref, shift_ref, pos_ref, freq_ref, w_ref, o_ref,
                   h_ref, cos_ref, sina_ref, sinb_ref, *, dq, rot, q_scale):
    j = pl.program_id(1)

    @pl.when(j == 0)
    def _():
        x = x_ref[...]
        ms = jnp.mean(x * x, axis=-1, keepdims=True)
        y = x * lax.rsqrt(ms + EPS) * g_ref[...]
        h_ref[...] = (y * (1.0 + scale_ref[0]) + shift_ref[0]).astype(BF16)
        ang = pos_ref[...].astype(F32) * freq_ref[...]
        d = lax.broadcasted_iota(jnp.int32, ang.shape, 1) % dq
        cs, sn = jnp.cos(ang), jnp.sin(ang)
        cos_ref[...] = jnp.where(d < rot, cs, 1.0)
        sina_ref[...] = jnp.where(d < rot // 2, -sn, 0.0)
        sinb_ref[...] = jnp.where((d >= rot // 2) & (d < rot), sn, 0.0)

    acc = jnp.dot(h_ref[...], w_ref[...], preferred_element_type=F32)

    @pl.when(j < 2)
    def _():
        sc = jnp.where(j == 0, q_scale, 1.0).astype(F32)
        cs, sa, sb = cos_ref[...] * sc, sina_ref[...] * sc, sinb_ref[...] * sc
        for t in range(acc.shape[1] // LANES):
            a = acc[:, t * LANES:(t + 1) * LANES]
            up = pltpu.roll(a, LANES - rot // 2, 1)
            dn = pltpu.roll(a, rot // 2, 1)
            o_ref[:, t * LANES:(t + 1) * LANES] = (a * cs + up * sa + dn * sb).astype(o_ref.dtype)

    @pl.when(j >= 2)
    def _():
        o_ref[...] = acc.astype(o_ref.dtype)


def _in_proj(x2, norm_g, mod3, pos2, freq_lane, w_in_bf, *, seq, dq, rot, tm=1024, tn=1024):
    m, d = x2.shape
    n = w_in_bf.shape[1]
    blocks_per_seq = seq // tm
    kern = functools.partial(_inproj_kernel, dq=dq, rot=rot, q_scale=LOG2E / math.sqrt(dq))
    return pl.pallas_call(
        kern,
        grid=(m // tm, n // tn),
        in_specs=[
            pl.BlockSpec((tm, d), lambda i, j: (i, 0)),
            pl.BlockSpec((1, d), lambda i, j: (0, 0)),
            pl.BlockSpec((1, 1, d), lambda i, j: ((i // blocks_per_seq) * 3 + 1, 0, 0)),
            pl.BlockSpec((1, 1, d), lambda i, j: ((i // blocks_per_seq) * 3 + 0, 0, 0)),
            pl.BlockSpec((tm, 1), lambda i, j: (i, 0)),
            pl.BlockSpec((1, LANES), lambda i, j: (0, 0)),
            pl.BlockSpec((d, tn), lambda i, j: (0, j)),
        ],
        out_specs=pl.BlockSpec((tm, tn), lambda i, j: (i, j)),
        out_shape=jax.ShapeDtypeStruct((m, n), BF16),
        scratch_shapes=[pltpu.VMEM((tm, d), BF16),
                        pltpu.VMEM((tm, LANES), F32),
                        pltpu.VMEM((tm, LANES), F32),
                        pltpu.VMEM((tm, LANES), F32)],
        compiler_params=pltpu.CompilerParams(
            dimension_semantics=("arbitrary", "arbitrary"), vmem_limit_bytes=VMEM_LIMIT),
        name="in_proj",
    )(x2, norm_g, mod3, mod3, pos2, freq_lane, w_in_bf)


def _attn_kernel(lam_ref, sg_ref, q_ref, k_ref, v_ref, g_ref, o_ref,
                 vt_ref, qq_ref, s_ref, mx_ref, m_ref, acc_ref, *, qb, kb, dq, chunk, lam_init):
    i = pl.program_id(2)
    dv = v_ref.shape[1]

    @pl.when(i == 0)
    def _():
        vt_ref[0:dv, :] = v_ref[...].astype(F32).T.astype(BF16)
        vt_ref[dv:, :] = jnp.ones((vt_ref.shape[0] - dv, vt_ref.shape[1]), BF16)

    q = q_ref[...]
    lane = lax.broadcasted_iota(jnp.int32, q.shape, 1)
    zero = jnp.zeros_like(q)
    qq_ref[0:qb, :] = jnp.where(lane < dq, q, zero)
    qq_ref[qb:, :] = jnp.where(lane >= dq, q, zero)

    m_ref[...] = jnp.full(m_ref.shape, NEG_BIG, F32)
    acc_ref[...] = jnp.zeros(acc_ref.shape, F32)

    def scores(j, slot):
        k0 = pl.multiple_of(j * kb, kb)
        s = lax.dot_general(k_ref[pl.ds(k0, kb), :], qq_ref[...], (((1,), (1,)), ((), ())),
                            preferred_element_type=F32)
        s_ref[slot] = s
        mx_ref[slot] = jnp.max(s, axis=0, keepdims=True)

    def softmax_pv(j, slot, masked):
        k0 = pl.multiple_of(j * kb, kb)
        for c in range(2 * qb // chunk):
            cols = slice(c * chunk, (c + 1) * chunk)
            s = s_ref[slot, :, cols]
            if masked:
                kpos = lax.broadcasted_iota(jnp.int32, s.shape, 0)
                qpos = (c * chunk) % qb + lax.broadcasted_iota(jnp.int32, s.shape, 1)
                s = jnp.where(kpos <= qpos, s, NEG_BIG)
                mx = jnp.max(s, axis=0, keepdims=True)
            else:
                mx = mx_ref[slot, :, cols]
            m_old = m_ref[:, cols]
            m_new = jnp.maximum(m_old, mx)
            alpha = jnp.exp2(m_old - m_new)
            p = jnp.exp2(s - m_new).astype(BF16)
            m_ref[:, cols] = m_new
            pv = jnp.dot(vt_ref[:, pl.ds(k0, kb)], p, preferred_element_type=F32)
            acc_ref[:, cols] = alpha * acc_ref[:, cols] + pv

    scores(0, 0)

    def pair(t, carry):
        j = 2 * t
        scores(j + 1, 1)
        softmax_pv(j, 0, masked=False)
        scores(j + 2, 0)
        softmax_pv(j + 1, 1, masked=False)
        return carry

    lax.fori_loop(0, i // 2, pair, 0)

    @pl.when(i % 2 == 1)
    def _():
        scores(i, 1)
        softmax_pv(i - 1, 0, masked=False)

    softmax_pv(i, i % 2, masked=True)

    lam = (jnp.exp(jnp.sum(lam_ref[0:1, :] * lam_ref[1:2, :], axis=-1, keepdims=True))
           - jnp.exp(jnp.sum(lam_ref[2:3, :] * lam_ref[3:4, :], axis=-1, keepdims=True))
           + lam_init)
    o_all = acc_ref[0:dv, :] / acc_ref[dv:dv + 1, :]
    o = o_all[:, :qb] - lam * o_all[:, qb:]
    ms = jnp.mean(o * o, axis=0, keepdims=True)
    on = (o * lax.rsqrt(ms + EPS)).T
    on = on * sg_ref[...] * (1.0 - lam_init)
    o_ref[...] = (on * _silu(g_ref[...].astype(F32))).astype(o_ref.dtype)


def _diff_attn(z, lam_params, subln_g, *, batch, seq, dq, dv, lam_init, qb=512):
    kb = qb
    nq = seq // qb
    h = N_HEADS
    k_col0 = h * 2 * dq // LANES
    v_col0 = 2 * k_col0
    g_col0 = v_col0 + h * dv // LANES
    kern = functools.partial(_attn_kernel, qb=qb, kb=kb, dq=dq, chunk=512, lam_init=lam_init)
    return pl.pallas_call(
        kern,
        grid=(batch, h, nq),
        in_specs=[
            pl.BlockSpec(lam_params.shape, lambda b, hh, i: (0, 0)),
            pl.BlockSpec((1, dv), lambda b, hh, i: (0, 0)),
            pl.BlockSpec((qb, 2 * dq), lambda b, hh, i: (b * nq + i, hh)),
            pl.BlockSpec((seq, 2 * dq), lambda b, hh, i: (b, k_col0 + hh)),
            pl.BlockSpec((seq, dv), lambda b, hh, i: (b, v_col0 + hh)),
            pl.BlockSpec((qb, dv), lambda b, hh, i: (b * nq + i, g_col0 + hh)),
        ],
        out_specs=pl.BlockSpec((qb, dv), lambda b, hh, i: (b * nq + i, hh)),
        out_shape=jax.ShapeDtypeStruct((batch * seq, h * dv), BF16),
        scratch_shapes=[pltpu.VMEM((dv + ONES_ROWS, seq), BF16),
                        pltpu.VMEM((2 * qb, 2 * dq), BF16),
                        pltpu.VMEM((2, kb, 2 * qb), F32),
                        pltpu.VMEM((2, 1, 2 * qb), F32),
                        pltpu.VMEM((1, 2 * qb), F32),
                        pltpu.VMEM((dv + ONES_ROWS, 2 * qb), F32)],
        compiler_params=pltpu.CompilerParams(
            dimension_semantics=("arbitrary", "arbitrary", "arbitrary"),
            vmem_limit_bytes=VMEM_LIMIT),
        name="diff_attn",
    )(lam_params, subln_g, z, z, z, z)


def _conv_kernel(ua_ref, ub_ref, ha_ref, hb_ref, gc_ref, dww_ref, dwb_ref, lng_ref, lnb_ref,
                 wpw_ref, bpw_ref, o_ref, sh_ref, conv_ref, *, ts, row_chunk, lane_chunk):
    i = pl.program_id(1)
    c = ua_ref.shape[1]

    def glu(a_ref, b_ref):
        return a_ref[...].astype(F32) * jax.nn.sigmoid(b_ref[...].astype(F32))

    hist = jnp.where(i > 0, glu(ha_ref, hb_ref), 0.0)
    ypad = jnp.concatenate([hist, glu(ua_ref, ub_ref)], axis=0)
    for r in range(SUBLANES):
        n_r = HALO + ts - (SUBLANES if r else 0)
        sh_ref[r, 0:n_r, :] = ypad[r:r + n_r, :]

    base = HALO - (CONV_WIDTH - 1)
    n_rc = ts // row_chunk
    for lc in range(c // lane_chunk):
        l0 = lc * lane_chunk

        def body(rc, carry, l0=l0):
            r0 = pl.multiple_of(rc * row_chunk, row_chunk)
            acc = jnp.zeros((row_chunk, lane_chunk), F32)
            for j in range(CONV_WIDTH):
                off = base + j
                a, r = off // SUBLANES, off % SUBLANES
                yv = sh_ref[r, pl.ds(r0 + a * SUBLANES, row_chunk), l0:l0 + lane_chunk]
                acc = acc + yv * dww_ref[j:j + 1, l0:l0 + lane_chunk]
            conv_ref[pl.ds(r0, row_chunk), l0:l0 + lane_chunk] = acc
            return carry

        lax.fori_loop(0, n_rc, body, 0)

    y = conv_ref[...] + dwb_ref[...]
    mu = jnp.mean(y, axis=-1, keepdims=True)
    var = jnp.mean(jnp.square(y - mu), axis=-1, keepdims=True)
    y = (y - mu) * lax.rsqrt(var + EPS) * lng_ref[...] + lnb_ref[...]
    y = _silu(y)
    y = jnp.dot(y.astype(BF16), wpw_ref[...], preferred_element_type=F32) + bpw_ref[...]
    o_ref[...] = (y * _silu(gc_ref[...].astype(F32))).astype(o_ref.dtype)


def _conv_mod(z, dw_w, dw_b, ln_g, ln_b, w_pw_bf, b_pw, *, batch, seq, c, ua_col0, ts=256):
    nt = seq // ts
    cb = c // c
    ua_blk, ub_blk, gc_blk = ua_col0 // c, ua_col0 // c + 1, ua_col0 // c + 2
    halo_per_ts = ts // HALO
    kern = functools.partial(_conv_kernel, ts=ts, row_chunk=32, lane_chunk=256)
    del cb

    def halo_map(blk):
        return lambda b, i: (jnp.maximum((b * nt + i) * halo_per_ts - 1, 0), blk)

    row = lambda b, i: (0, 0)
    return pl.pallas_call(
        kern,
        grid=(batch, nt),
        in_specs=[
            pl.BlockSpec((ts, c), lambda b, i: (b * nt + i, ua_blk)),
            pl.BlockSpec((ts, c), lambda b, i: (b * nt + i, ub_blk)),
            pl.BlockSpec((HALO, c), halo_map(ua_blk)),
            pl.BlockSpec((HALO, c), halo_map(ub_blk)),
            pl.BlockSpec((ts, c), lambda b, i: (b * nt + i, gc_blk)),
            pl.BlockSpec((CONV_WIDTH, c), row),
            pl.BlockSpec((1, c), row),
            pl.BlockSpec((1, c), row),
            pl.BlockSpec((1, c), row),
            pl.BlockSpec((c, c), row),
            pl.BlockSpec((1, c), row),
        ],
        out_specs=pl.BlockSpec((ts, c), lambda b, i: (b * nt + i, 0)),
        out_shape=jax.ShapeDtypeStruct((batch * seq, c), BF16),
        scratch_shapes=[pltpu.VMEM((SUBLANES, HALO + ts, c), F32),
                        pltpu.VMEM((ts, c), F32)],
        compiler_params=pltpu.CompilerParams(
            dimension_semantics=("arbitrary", "arbitrary"), vmem_limit_bytes=VMEM_LIMIT),
        name="conv_mod",
    )(z, z, z, z, z, dw_w, dw_b, ln_g, ln_b, w_pw_bf, b_pw)


def _outproj_kernel(ya_ref, yc_ref, wa_ref, wc_ref, x_ref, gate_ref, fg_ref, o_ref, *, final_norm):
    mixed = jnp.dot(ya_ref[...], wa_ref[...], preferred_element_type=F32)
    mixed = mixed + jnp.dot(yc_ref[...], wc_ref[...], preferred_element_type=F32)
    xn = x_ref[...] + gate_ref[0] * mixed
    if final_norm:
        ms = jnp.mean(xn * xn, axis=-1, keepdims=True)
        xn = xn * lax.rsqrt(ms + EPS) * fg_ref[...]
    o_ref[...] = xn


def _out_proj(y_attn, y_conv, w_out_bf, x2, mod3, final_g, *, seq, final_norm, tm=512):
    m, d = x2.shape
    da, dc = y_attn.shape[1], y_conv.shape[1]
    blocks_per_seq = seq // tm
    return pl.pallas_call(
        functools.partial(_outproj_kernel, final_norm=final_norm),
        grid=(m // tm,),
        in_specs=[
            pl.BlockSpec((tm, da), lambda i: (i, 0)),
            pl.BlockSpec((tm, dc), lambda i: (i, 0)),
            pl.BlockSpec((da, d), lambda i: (0, 0)),
            pl.BlockSpec((dc, d), lambda i: (da // dc, 0)),
            pl.BlockSpec((tm, d), lambda i: (i, 0)),
            pl.BlockSpec((1, 1, d), lambda i: ((i // blocks_per_seq) * 3 + 2, 0, 0)),
            pl.BlockSpec((1, d), lambda i: (0, 0)),
        ],
        out_specs=pl.BlockSpec((tm, d), lambda i: (i, 0)),
        out_shape=jax.ShapeDtypeStruct((m, d), F32),
        compiler_params=pltpu.CompilerParams(
            dimension_semantics=("arbitrary",), vmem_limit_bytes=VMEM_LIMIT),
        name="out_proj",
    )(y_attn, y_conv, w_out_bf, w_out_bf, x2, mod3, final_g)


def kernel(x, c, positions, norm_g, w_ada, b_ada, w_in, lambda_q1, lambda_k1, lambda_q2, lambda_k2,
           subln_g, conv_dw_w, conv_dw_b, conv_ln_g, conv_ln_b, w_pw, b_pw, w_out, final_g):
    batch, seq, d = x.shape
    depth = w_in.shape[0]
    d_attn = d // 2
    d_conv = d - d_attn
    dv = d_attn // N_HEADS
    dq = dv // 2
    rot = dq // 4

    inv_freq = ROPE_THETA ** (-jnp.arange(0, rot, 2, dtype=F32) / rot)
    freq_lane = jnp.tile(inv_freq, LANES // (rot // 2)).reshape(1, LANES)
    pos2 = positions.reshape(batch * seq, 1)
    c_pad = jnp.zeros((SUBLANES, d), F32).at[:batch].set(c)

    x2 = x.reshape(batch * seq, d)
    for l in range(depth):
        lam_init = 0.8 - 0.6 * math.exp(-0.3 * l)
        mod = _ada_mod(c_pad, w_ada[l], b_ada[l].reshape(1, -1))
        mod3 = mod[:batch].reshape(batch * 3, 1, d)
        z = _in_proj(x2, norm_g[l].reshape(1, d), mod3, pos2, freq_lane, w_in[l].astype(BF16),
                     seq=seq, dq=dq, rot=rot)
        lam_params = jnp.stack([lambda_q1[l], lambda_k1[l], lambda_q2[l], lambda_k2[l]]).astype(F32)
        y_attn = _diff_attn(z, lam_params, subln_g[l].reshape(1, dv),
                            batch=batch, seq=seq, dq=dq, dv=dv, lam_init=lam_init)
        y_conv = _conv_mod(z, conv_dw_w[l], conv_dw_b[l].reshape(1, -1), conv_ln_g[l].reshape(1, -1),
                           conv_ln_b[l].reshape(1, -1), w_pw[l].astype(BF16), b_pw[l].reshape(1, -1),
                           batch=batch, seq=seq, c=d_conv, ua_col0=4 * d_attn)
        x2 = _out_proj(y_attn, y_conv, w_out[l].astype(BF16), x2, mod3, final_g.reshape(1, d),
                       seq=seq, final_norm=(l == depth - 1))
    return x2.reshape(batch, seq, d)
```

```python
import functools
import math

import jax
import jax.numpy as jnp
from jax import lax
from jax.experimental import pallas as pl
from jax.experimental.pallas import tpu as pltpu

F32 = jnp.float32
BF16 = jnp.bfloat16

N_HEADS = 8
CONV_WIDTH = 31
ROPE_THETA = 500000.0
EPS = 1e-6
LANES = 128
SUBLANES = 8
HALO = 32
NEG_BIG = -1e30
ONES_ROWS = 16
LOG2E = math.log2(math.e)
N_SLOTS = 3
VMEM_LIMIT = 56 * 1024 * 1024


def _aligned(start, multiple):
    return start if isinstance(start, int) else pl.multiple_of(start, multiple)


def _silu(t):
    return t * jax.nn.sigmoid(t)


def _ada_kernel(c_ref, w_ref, b_ref, o_ref):
    ca = _silu(c_ref[...])
    o_ref[...] = jnp.dot(ca.astype(BF16), w_ref[...].astype(BF16),
                         preferred_element_type=F32) + b_ref[...]


def _ada_mod(c_pad, w_ada, b_ada, tn=1024):
    rows, d = c_pad.shape
    n = w_ada.shape[1]
    return pl.pallas_call(
        _ada_kernel,
        grid=(n // tn,),
        in_specs=[pl.BlockSpec((rows, d), lambda j: (0, 0)),
                  pl.BlockSpec((d, tn), lambda j: (0, j)),
                  pl.BlockSpec((1, tn), lambda j: (0, j))],
        out_specs=pl.BlockSpec((rows, tn), lambda j: (0, j)),
        out_shape=jax.ShapeDtypeStruct((rows, n), F32),
        compiler_params=pltpu.CompilerParams(
            dimension_semantics=("arbitrary",), vmem_limit_bytes=VMEM_LIMIT),
        name="ada_mod",
    )(c_pad, w_ada, b_ada)


def _rope_tab_kernel(pos_ref, freq_ref, cos_ref, sin_ref):
    ang = pos_ref[...].astype(F32) * freq_ref[...]
    cos_ref[...] = jnp.cos(ang)
    sin_ref[...] = jnp.sin(ang)


def _rope_tables(positions, *, dq, rot):
    n = positions.size
    nf = rot // 2
    inv_freq = ROPE_THETA ** (-jnp.arange(0, rot, 2, dtype=F32) / rot)
    freq_lane = jnp.tile(inv_freq, LANES // nf).reshape(1, LANES)
    pos_rep = jnp.repeat(positions.reshape(-1), nf).reshape(n * nf // LANES, LANES)
    cos, sin = pl.pallas_call(
        _rope_tab_kernel,
        out_shape=[jax.ShapeDtypeStruct(pos_rep.shape, F32)] * 2,
        name="rope_tab",
    )(pos_rep, freq_lane)
    cos, sin = cos.reshape(n, nf), sin.reshape(n, nf)
    one, zero = jnp.ones((n, dq - rot), F32), jnp.zeros((n, dq - rot), F32)
    z8 = jnp.zeros((n, nf), F32)
    tab_c = jnp.tile(jnp.concatenate([cos, cos, one], axis=1), (1, LANES // dq))
    tab_up = jnp.tile(jnp.concatenate([sin, z8, zero], axis=1), (1, LANES // dq))
    tab_dn = jnp.tile(jnp.concatenate([z8, sin, zero], axis=1), (1, LANES // dq))
    return tab_c, tab_up, tab_dn


def _inproj_kernel(x_ref, g_ref, scale_ref, shift_ref, tc_ref, tu_ref, td_ref, w_ref, o_ref,
                   h_ref, *, rot, q_scale, n_rope_blocks, chunk, norm_rows_per_iter):
    j = pl.program_id(1)

    @pl.when(j == 0)
    def _():
        gs = g_ref[...] * (1.0 + scale_ref[0])
        sh = shift_ref[0]

        def norm_rows(r, carry):
            rows = pl.ds(pl.multiple_of(r * norm_rows_per_iter, norm_rows_per_iter), norm_rows_per_iter)
            x = x_ref[rows, :]
            ms = jnp.mean(x * x, axis=-1, keepdims=True)
            h_ref[rows, :] = (x * lax.rsqrt(ms + EPS) * gs + sh).astype(BF16)
            return carry

        lax.fori_loop(0, x_ref.shape[0] // norm_rows_per_iter, norm_rows, 0)

    @pl.when(j < n_rope_blocks)
    def _():
        sc = jnp.where(j < n_rope_blocks // 2, q_scale, 1.0).astype(F32)
        cs, su, sd = tc_ref[...] * sc, tu_ref[...] * sc, td_ref[...] * sc
        for c in range(w_ref.shape[1] // chunk):
            acc = jnp.dot(h_ref[...], w_ref[:, c * chunk:(c + 1) * chunk], preferred_element_type=F32)
            for t in range(chunk // LANES):
                a = acc[:, t * LANES:(t + 1) * LANES]
                up = pltpu.roll(a, LANES - rot // 2, 1)
                dn = pltpu.roll(a, rot // 2, 1)
                col = c * chunk + t * LANES
                o_ref[:, col:col + LANES] = (a * cs - up * su + dn * sd).astype(o_ref.dtype)

    @pl.when(j >= n_rope_blocks)
    def _():
        o_ref[...] = jnp.dot(h_ref[...], w_ref[...], preferred_element_type=F32).astype(o_ref.dtype)


def _in_proj(x2, norm_g, mod3, tabs, w_in_bf, *, seq, dq, rot, d_qk, tm=1024, tn=1024):
    m, d = x2.shape
    n = w_in_bf.shape[1]
    blocks_per_seq = seq // tm
    kern = functools.partial(_inproj_kernel, rot=rot, q_scale=LOG2E / math.sqrt(dq),
                             n_rope_blocks=d_qk // tn, chunk=256, norm_rows_per_iter=16)
    tab_spec = pl.BlockSpec((tm, LANES), lambda i, j: (i, 0))
    return pl.pallas_call(
        kern,
        grid=(m // tm, n // tn),
        in_specs=[
            pl.BlockSpec((tm, d), lambda i, j: (i, 0)),
            pl.BlockSpec((1, d), lambda i, j: (0, 0)),
            pl.BlockSpec((1, 1, d), lambda i, j: ((i // blocks_per_seq) * 3 + 1, 0, 0)),
            pl.BlockSpec((1, 1, d), lambda i, j: ((i // blocks_per_seq) * 3 + 0, 0, 0)),
            tab_spec, tab_spec, tab_spec,
            pl.BlockSpec((d, tn), lambda i, j: (0, j)),
        ],
        out_specs=pl.BlockSpec((tm, tn), lambda i, j: (i, j)),
        out_shape=jax.ShapeDtypeStruct((m, n), BF16),
        scratch_shapes=[pltpu.VMEM((tm, d), BF16)],
        compiler_params=pltpu.CompilerParams(
            dimension_semantics=("arbitrary", "arbitrary"), vmem_limit_bytes=VMEM_LIMIT),
        name="in_proj",
    )(x2, norm_g, mod3, mod3, *tabs, w_in_bf)


def _attn_kernel(lam_ref, sg_ref, q_ref, k_ref, v_ref, g_ref, o_ref,
                 vt_ref, qq_ref, s_ref, p_ref, al_ref, m_ref, acc_ref, *, qb, kb, dq, row_chunk, lam_init):
    seq, dv = v_ref.shape
    nq = seq // qb
    n_items = nq * (nq + 1) // 2

    vt_ref[0:dv, :] = v_ref[...].astype(F32).T.astype(BF16)
    vt_ref[dv:, :] = jnp.ones((vt_ref.shape[0] - dv, seq), BF16)

    lam = (jnp.exp(jnp.sum(lam_ref[0:1, :] * lam_ref[1:2, :], axis=-1, keepdims=True))
           - jnp.exp(jnp.sum(lam_ref[2:3, :] * lam_ref[3:4, :], axis=-1, keepdims=True))
           + lam_init)

    def build_qq(qi):
        q = q_ref[pl.ds(_aligned(qi * qb, qb), qb), :]
        lane = lax.broadcasted_iota(jnp.int32, q.shape, 1)
        zero = jnp.zeros_like(q)
        qq_ref[0:qb, :] = jnp.where(lane < dq, q, zero)
        qq_ref[qb:, :] = jnp.where(lane >= dq, q, zero)

    def scores(j, slot):
        k0 = _aligned(j * kb, kb)
        s = lax.dot_general(k_ref[pl.ds(k0, kb), :], qq_ref[...], (((1,), (1,)), ((), ())),
                            preferred_element_type=F32)
        s_ref[slot] = s

    def softmax(slot, masked):
        for c in range(2):
            cols = slice(c * qb, (c + 1) * qb)

            def chunk(r):
                s = s_ref[slot, r:r + row_chunk, cols]
                if masked:
                    kpos = r + lax.broadcasted_iota(jnp.int32, s.shape, 0)
                    qpos = lax.broadcasted_iota(jnp.int32, s.shape, 1)
                    s = jnp.where(kpos <= qpos, s, NEG_BIG)
                return s

            part = None
            for r in range(0, kb, row_chunk):
                t = jnp.max(chunk(r).reshape(row_chunk // SUBLANES, SUBLANES, qb), axis=0)
                part = t if part is None else jnp.maximum(part, t)
            m_old = m_ref[:, cols]
            m_new = jnp.maximum(m_old, jnp.max(part, axis=0, keepdims=True))
            al_ref[slot, :, cols] = jnp.exp2(m_old - m_new)
            m_ref[:, cols] = m_new
            for r in range(0, kb, row_chunk):
                p_ref[slot, r:r + row_chunk, cols] = jnp.exp2(chunk(r) - m_new).astype(BF16)

    def pv_update(j, slot):
        k0 = _aligned(j * kb, kb)
        pv = jnp.dot(vt_ref[:, pl.ds(k0, kb)], p_ref[slot], preferred_element_type=F32)
        acc_ref[...] = al_ref[slot] * acc_ref[...] + pv

    def finalize(qi):
        rows = pl.ds(_aligned(qi * qb, qb), qb)
        o_all = acc_ref[0:dv, :] / acc_ref[dv:dv + 1, :]
        o = o_all[:, :qb] - lam * o_all[:, qb:]
        ms = jnp.mean(o * o, axis=0, keepdims=True)
        on = (o * lax.rsqrt(ms + EPS)).T
        on = on * sg_ref[...] * (1.0 - lam_init)
        o_ref[rows, :] = (on * _silu(g_ref[rows, :].astype(F32))).astype(o_ref.dtype)

    def item(qi, j, jp, slot):
        other = 1 - slot

        def middle():
            softmax(slot, masked=False)
            pv_update(jp, other)
            scores(j + 1, other)

        def row_start():
            softmax(slot, masked=False)
            pv_update(jp, other)
            scores(1, other)
            finalize(qi - 1)
            acc_ref[...] = jnp.zeros(acc_ref.shape, F32)

        def diag():
            build_qq(jnp.minimum(qi + 1, nq - 1))
            softmax(slot, masked=True)
            pv_update(jp, other)
            scores(0, other)
            m_ref[...] = jnp.full(m_ref.shape, NEG_BIG, F32)

        is_diag = j == qi
        lax.switch(jnp.where(is_diag, 2, jnp.where(j == 0, 1, 0)), [middle, row_start, diag])
        return jnp.where(is_diag, qi + 1, qi), jnp.where(is_diag, 0, j + 1), j

    del item
    pairs = [(qi, j) for qi in range(nq) for j in range(qi + 1)]
    build_qq(0)
    m_ref[...] = jnp.full(m_ref.shape, NEG_BIG, F32)
    acc_ref[...] = jnp.zeros(acc_ref.shape, F32)
    scores(0, 0)
    for n, (qi, j) in enumerate(pairs):
        slot, prev, nxt = n % N_SLOTS, (n - 1) % N_SLOTS, (n + 1) % N_SLOTS
        if j == qi and qi + 1 < nq:
            build_qq(qi + 1)
        softmax(slot, masked=(j == qi))
        if n > 0:
            pv_update(pairs[n - 1][1], prev)
        if n + 1 < len(pairs):
            scores(pairs[n + 1][1], nxt)
        if j == 0 and qi > 0:
            finalize(qi - 1)
            acc_ref[...] = jnp.zeros(acc_ref.shape, F32)
        if j == qi:
            m_ref[...] = jnp.full(m_ref.shape, NEG_BIG, F32)
    pv_update(pairs[-1][1], (len(pairs) - 1) % N_SLOTS)
    finalize(nq - 1)


def _diff_attn(z, lam_params, subln_g, *, batch, seq, dq, dv, lam_init, qb=512):
    kb = qb
    nq = seq // qb
    assert (nq * (nq + 1) // 2) % 2 == 0, "the pair loop needs an even number of (q block, k block) pairs"
    h = N_HEADS
    k_col0 = h * 2 * dq // LANES
    v_col0 = 2 * k_col0
    g_col0 = v_col0 + h * dv // LANES
    kern = functools.partial(_attn_kernel, qb=qb, kb=kb, dq=dq, row_chunk=32, lam_init=lam_init)
    return pl.pallas_call(
        kern,
        grid=(batch, h),
        in_specs=[
            pl.BlockSpec(lam_params.shape, lambda b, hh: (0, 0)),
            pl.BlockSpec((1, dv), lambda b, hh: (0, 0)),
            pl.BlockSpec((seq, 2 * dq), lambda b, hh: (b, hh)),
            pl.BlockSpec((seq, 2 * dq), lambda b, hh: (b, k_col0 + hh)),
            pl.BlockSpec((seq, dv), lambda b, hh: (b, v_col0 + hh)),
            pl.BlockSpec((seq, dv), lambda b, hh: (b, g_col0 + hh)),
        ],
        out_specs=pl.BlockSpec((seq, dv), lambda b, hh: (b, hh)),
        out_shape=jax.ShapeDtypeStruct((batch * seq, h * dv), BF16),
        scratch_shapes=[pltpu.VMEM((dv + ONES_ROWS, seq), BF16),
                        pltpu.VMEM((2 * qb, 2 * dq), BF16),
                        pltpu.VMEM((N_SLOTS, kb, 2 * qb), F32),
                        pltpu.VMEM((N_SLOTS, kb, 2 * qb), BF16),
                        pltpu.VMEM((N_SLOTS, 1, 2 * qb), F32),
                        pltpu.VMEM((1, 2 * qb), F32),
                        pltpu.VMEM((dv + ONES_ROWS, 2 * qb), F32)],
        compiler_params=pltpu.CompilerParams(
            dimension_semantics=("arbitrary", "arbitrary"),
            vmem_limit_bytes=VMEM_LIMIT),
        name="diff_attn",
    )(lam_params, subln_g, z, z, z, z)


def _conv_kernel(ua_ref, ub_ref, ha_ref, hb_ref, gc_ref, dww_ref, dwb_ref, lng_ref, lnb_ref,
                 wpw_ref, bpw_ref, o_ref, sh_ref, wb_ref, conv_ref, *, ts, row_chunk, lane_chunk):
    i = pl.program_id(1)
    c = ua_ref.shape[1]

    @pl.when((pl.program_id(0) == 0) & (i == 0))
    def _():
        for j in range(CONV_WIDTH):
            wb_ref[j] = jnp.broadcast_to(dww_ref[j:j + 1, :], (SUBLANES, c))

    def glu(a_ref, b_ref):
        return a_ref[...].astype(F32) * jax.nn.sigmoid(b_ref[...].astype(F32))

    hist = jnp.where(i > 0, glu(ha_ref, hb_ref), 0.0)
    ypad = jnp.concatenate([hist, glu(ua_ref, ub_ref)], axis=0)
    for r in range(SUBLANES):
        n_r = HALO + ts - (SUBLANES if r else 0)
        sh_ref[r, 0:n_r, :] = ypad[r:r + n_r, :]

    base = HALO - (CONV_WIDTH - 1)
    n_rc = ts // row_chunk
    for lc in range(c // lane_chunk):
        l0 = lc * lane_chunk

        def body(rc, carry, l0=l0):
            r0 = pl.multiple_of(rc * row_chunk, row_chunk)
            acc = jnp.zeros((row_chunk, lane_chunk), F32)
            for j in range(CONV_WIDTH):
                off = base + j
                a, r = off // SUBLANES, off % SUBLANES
                yv = sh_ref[r, pl.ds(r0 + a * SUBLANES, row_chunk), l0:l0 + lane_chunk]
                w = wb_ref[j, :, l0:l0 + lane_chunk]
                acc = acc + (yv.reshape(row_chunk // SUBLANES, SUBLANES, lane_chunk) * w
                             ).reshape(row_chunk, lane_chunk)
            conv_ref[pl.ds(r0, row_chunk), l0:l0 + lane_chunk] = acc
            return carry

        lax.fori_loop(0, n_rc, body, 0)

    y = conv_ref[...] + dwb_ref[...]
    mu = jnp.mean(y, axis=-1, keepdims=True)
    var = jnp.mean(jnp.square(y - mu), axis=-1, keepdims=True)
    y = (y - mu) * lax.rsqrt(var + EPS) * lng_ref[...] + lnb_ref[...]
    y = _silu(y)
    y = jnp.dot(y.astype(BF16), wpw_ref[...], preferred_element_type=F32) + bpw_ref[...]
    o_ref[...] = (y * _silu(gc_ref[...].astype(F32))).astype(o_ref.dtype)


def _conv_mod(z, dw_w, dw_b, ln_g, ln_b, w_pw_bf, b_pw, *, batch, seq, c, ua_col0, ts=256):
    nt = seq // ts
    ua_blk, ub_blk, gc_blk = ua_col0 // c, ua_col0 // c + 1, ua_col0 // c + 2
    halo_per_ts = ts // HALO
    kern = functools.partial(_conv_kernel, ts=ts, row_chunk=32, lane_chunk=256)

    def halo_map(blk):
        return lambda b, i: (jnp.maximum((b * nt + i) * halo_per_ts - 1, 0), blk)

    row = lambda b, i: (0, 0)
    return pl.pallas_call(
        kern,
        grid=(batch, nt),
        in_specs=[
            pl.BlockSpec((ts, c), lambda b, i: (b * nt + i, ua_blk)),
            pl.BlockSpec((ts, c), lambda b, i: (b * nt + i, ub_blk)),
            pl.BlockSpec((HALO, c), halo_map(ua_blk)),
            pl.BlockSpec((HALO, c), halo_map(ub_blk)),
            pl.BlockSpec((ts, c), lambda b, i: (b * nt + i, gc_blk)),
            pl.BlockSpec((CONV_WIDTH, c), row),
            pl.BlockSpec((1, c), row),
            pl.BlockSpec((1, c), row),
            pl.BlockSpec((1, c), row),
            pl.BlockSpec((c, c), row),
            pl.BlockSpec((1, c), row),
        ],
        out_specs=pl.BlockSpec((ts, c), lambda b, i: (b * nt + i, 0)),
        out_shape=jax.ShapeDtypeStruct((batch * seq, c), BF16),
        scratch_shapes=[pltpu.VMEM((SUBLANES, HALO + ts, c), F32),
                        pltpu.VMEM((CONV_WIDTH, SUBLANES, c), F32),
                        pltpu.VMEM((ts, c), F32)],
        compiler_params=pltpu.CompilerParams(
            dimension_semantics=("arbitrary", "arbitrary"), vmem_limit_bytes=VMEM_LIMIT),
        name="conv_mod",
    )(z, z, z, z, z, dw_w, dw_b, ln_g, ln_b, w_pw_bf, b_pw)


def _outproj_kernel(ya_ref, yc_ref, wa_ref, wc_ref, x_ref, gate_ref, fg_ref, o_ref, *, final_norm):
    mixed = jnp.dot(ya_ref[...], wa_ref[...], preferred_element_type=F32)
    mixed = mixed + jnp.dot(yc_ref[...], wc_ref[...], preferred_element_type=F32)
    xn = x_ref[...] + gate_ref[0] * mixed
    if final_norm:
        ms = jnp.mean(xn * xn, axis=-1, keepdims=True)
        xn = xn * lax.rsqrt(ms + EPS) * fg_ref[...]
    o_ref[...] = xn


def _out_proj(y_attn, y_conv, w_out_bf, x2, mod3, final_g, *, seq, final_norm, tm=512):
    m, d = x2.shape
    da, dc = y_attn.shape[1], y_conv.shape[1]
    blocks_per_seq = seq // tm
    return pl.pallas_call(
        functools.partial(_outproj_kernel, final_norm=final_norm),
        grid=(m // tm,),
        in_specs=[
            pl.BlockSpec((tm, da), lambda i: (i, 0)),
            pl.BlockSpec((tm, dc), lambda i: (i, 0)),
            pl.BlockSpec((da, d), lambda i: (0, 0)),
            pl.BlockSpec((dc, d), lambda i: (da // dc, 0)),
            pl.BlockSpec((tm, d), lambda i: (i, 0)),
            pl.BlockSpec((1, 1, d), lambda i: ((i // blocks_per_seq) * 3 + 2, 0, 0)),
            pl.BlockSpec((1, d), lambda i: (0, 0)),
        ],
        out_specs=pl.BlockSpec((tm, d), lambda i: (i, 0)),
        out_shape=jax.ShapeDtypeStruct((m, d), F32),
        compiler_params=pltpu.CompilerParams(
            dimension_semantics=("arbitrary",), vmem_limit_bytes=VMEM_LIMIT),
        name="out_proj",
    )(y_attn, y_conv, w_out_bf, w_out_bf, x2, mod3, final_g)


def kernel(x, c, positions, norm_g, w_ada, b_ada, w_in, lambda_q1, lambda_k1, lambda_q2, lambda_k2,
           subln_g, conv_dw_w, conv_dw_b, conv_ln_g, conv_ln_b, w_pw, b_pw, w_out, final_g):
    batch, seq, d = x.shape
    depth = w_in.shape[0]
    d_attn = d // 2
    d_conv = d - d_attn
    dv = d_attn // N_HEADS
    dq = dv // 2
    rot = dq // 4

    tabs = _rope_tables(positions, dq=dq, rot=rot)
    c_pad = jnp.zeros((SUBLANES, d), F32).at[:batch].set(c)

    x2 = x.reshape(batch * seq, d)
    for l in range(depth):
        lam_init = 0.8 - 0.6 * math.exp(-0.3 * l)
        mod = _ada_mod(c_pad, w_ada[l], b_ada[l].reshape(1, -1))
        mod3 = mod[:batch].reshape(batch * 3, 1, d)
        z = _in_proj(x2, norm_g[l].reshape(1, d), mod3, tabs, w_in[l].astype(BF16),
                     seq=seq, dq=dq, rot=rot, d_qk=2 * d_attn)
        lam_params = jnp.stack([lambda_q1[l], lambda_k1[l], lambda_q2[l], lambda_k2[l]]).astype(F32)
        y_attn = _diff_attn(z, lam_params, subln_g[l].reshape(1, dv),
                            batch=batch, seq=seq, dq=dq, dv=dv, lam_init=lam_init)
        y_conv = _conv_mod(z, conv_dw_w[l], conv_dw_b[l].reshape(1, -1), conv_ln_g[l].reshape(1, -1),
                           conv_ln_b[l].reshape(1, -1), w_pw[l].astype(BF16), b_pw[l].reshape(1, -1),
                           batch=batch, seq=seq, c=d_conv, ua_col0=4 * d_attn)
        x2 = _out_proj(y_attn, y_conv, w_out[l].astype(BF16), x2, mod3, final_g.reshape(1, d),
                       seq=seq, final_norm=(l == depth - 1))
    return x2.reshape(batch, seq, d)
```

```python
import functools
import math

import jax
import jax.numpy as jnp
from jax import lax
from jax.experimental import pallas as pl
from jax.experimental.pallas import tpu as pltpu

F32 = jnp.float32
BF16 = jnp.bfloat16

N_HEADS = 8
CONV_WIDTH = 31
ROPE_THETA = 500000.0
EPS = 1e-6
LANES = 128
SUBLANES = 8
HALO = 32
NEG_BIG = -1e30
ONES_ROWS = 16
LOG2E = math.log2(math.e)
N_SLOTS = 3
VMEM_LIMIT = 56 * 1024 * 1024


def _aligned(start, multiple):
    return start if isinstance(start, int) else pl.multiple_of(start, multiple)


def _silu(t):
    return t * jax.nn.sigmoid(t)


def _ada_kernel(c_ref, w_ref, b_ref, o_ref):
    ca = _silu(c_ref[...])
    o_ref[...] = jnp.dot(ca.astype(BF16), w_ref[...].astype(BF16),
                         preferred_element_type=F32) + b_ref[...]


def _ada_mod(c_pad, w_ada, b_ada, tn=1024):
    rows, d = c_pad.shape
    n = w_ada.shape[1]
    return pl.pallas_call(
        _ada_kernel,
        grid=(n // tn,),
        in_specs=[pl.BlockSpec((rows, d), lambda j: (0, 0)),
                  pl.BlockSpec((d, tn), lambda j: (0, j)),
                  pl.BlockSpec((1, tn), lambda j: (0, j))],
        out_specs=pl.BlockSpec((rows, tn), lambda j: (0, j)),
        out_shape=jax.ShapeDtypeStruct((rows, n), F32),
        compiler_params=pltpu.CompilerParams(
            dimension_semantics=("arbitrary",), vmem_limit_bytes=VMEM_LIMIT),
        name="ada_mod",
    )(c_pad, w_ada, b_ada)


def _rope_tab_kernel(pos_ref, freq_ref, cos_ref, sin_ref):
    ang = pos_ref[...].astype(F32) * freq_ref[...]
    cos_ref[...] = jnp.cos(ang)
    sin_ref[...] = jnp.sin(ang)


def _rope_tables(positions, *, dq, rot):
    n = positions.size
    nf = rot // 2
    inv_freq = ROPE_THETA ** (-jnp.arange(0, rot, 2, dtype=F32) / rot)
    freq_lane = jnp.tile(inv_freq, LANES // nf).reshape(1, LANES)
    pos_rep = jnp.repeat(positions.reshape(-1), nf).reshape(n * nf // LANES, LANES)
    cos, sin = pl.pallas_call(
        _rope_tab_kernel,
        out_shape=[jax.ShapeDtypeStruct(pos_rep.shape, F32)] * 2,
        name="rope_tab",
    )(pos_rep, freq_lane)
    cos, sin = cos.reshape(n, nf), sin.reshape(n, nf)
    one, zero = jnp.ones((n, dq - rot), F32), jnp.zeros((n, dq - rot), F32)
    z8 = jnp.zeros((n, nf), F32)
    tab_c = jnp.tile(jnp.concatenate([cos, cos, one], axis=1), (1, LANES // dq))
    tab_up = jnp.tile(jnp.concatenate([sin, z8, zero], axis=1), (1, LANES // dq))
    tab_dn = jnp.tile(jnp.concatenate([z8, sin, zero], axis=1), (1, LANES // dq))
    return tab_c, tab_up, tab_dn


def _inproj_kernel(x_ref, g_ref, scale_ref, shift_ref, tc_ref, tu_ref, td_ref, w_ref, o_ref,
                   h_ref, *, rot, q_scale, n_rope_blocks, chunk, norm_rows_per_iter):
    j = pl.program_id(1)

    @pl.when(j == 0)
    def _():
        gs = g_ref[...] * (1.0 + scale_ref[0])
        sh = shift_ref[0]

        def norm_rows(r, carry):
            rows = pl.ds(pl.multiple_of(r * norm_rows_per_iter, norm_rows_per_iter), norm_rows_per_iter)
            x = x_ref[rows, :]
            ms = jnp.mean(x * x, axis=-1, keepdims=True)
            h_ref[rows, :] = (x * lax.rsqrt(ms + EPS) * gs + sh).astype(BF16)
            return carry

        lax.fori_loop(0, x_ref.shape[0] // norm_rows_per_iter, norm_rows, 0, unroll=8)

    @pl.when(j < n_rope_blocks)
    def _():
        sc = jnp.where(j < n_rope_blocks // 2, q_scale, 1.0).astype(F32)
        cs, su, sd = tc_ref[...] * sc, tu_ref[...] * sc, td_ref[...] * sc
        for c in range(w_ref.shape[1] // chunk):
            acc = jnp.dot(h_ref[...], w_ref[:, c * chunk:(c + 1) * chunk], preferred_element_type=F32)
            for t in range(chunk // LANES):
                a = acc[:, t * LANES:(t + 1) * LANES]
                up = pltpu.roll(a, LANES - rot // 2, 1)
                dn = pltpu.roll(a, rot // 2, 1)
                col = c * chunk + t * LANES
                o_ref[:, col:col + LANES] = (a * cs - up * su + dn * sd).astype(o_ref.dtype)

    @pl.when(j >= n_rope_blocks)
    def _():
        o_ref[...] = jnp.dot(h_ref[...], w_ref[...], preferred_element_type=F32).astype(o_ref.dtype)


def _in_proj(x2, norm_g, mod3, tabs, w_in_bf, *, seq, dq, rot, d_qk, tm=1024, tn=1024):
    m, d = x2.shape
    n = w_in_bf.shape[1]
    blocks_per_seq = seq // tm
    kern = functools.partial(_inproj_kernel, rot=rot, q_scale=LOG2E / math.sqrt(dq),
                             n_rope_blocks=d_qk // tn, chunk=256, norm_rows_per_iter=16)
    tab_spec = pl.BlockSpec((tm, LANES), lambda i, j: (i, 0))
    return pl.pallas_call(
        kern,
        grid=(m // tm, n // tn),
        in_specs=[
            pl.BlockSpec((tm, d), lambda i, j: (i, 0)),
            pl.BlockSpec((1, d), lambda i, j: (0, 0)),
            pl.BlockSpec((1, 1, d), lambda i, j: ((i // blocks_per_seq) * 3 + 1, 0, 0)),
            pl.BlockSpec((1, 1, d), lambda i, j: ((i // blocks_per_seq) * 3 + 0, 0, 0)),
            tab_spec, tab_spec, tab_spec,
            pl.BlockSpec((d, tn), lambda i, j: (0, j)),
        ],
        out_specs=pl.BlockSpec((tm, tn), lambda i, j: (i, j)),
        out_shape=jax.ShapeDtypeStruct((m, n), BF16),
        scratch_shapes=[pltpu.VMEM((tm, d), BF16)],
        compiler_params=pltpu.CompilerParams(
            dimension_semantics=("arbitrary", "arbitrary"), vmem_limit_bytes=VMEM_LIMIT),
        name="in_proj",
    )(x2, norm_g, mod3, mod3, *tabs, w_in_bf)


def _attn_kernel(lam_ref, sg_ref, q_ref, k_ref, v_ref, g_ref, o_ref,
                 vt_ref, qq_ref, s_ref, p_ref, al_ref, m_ref, acc_ref, *, qb, kb, dq, row_chunk, lam_init):
    seq, dv = v_ref.shape
    nq = seq // qb
    n_items = nq * (nq + 1) // 2

    vt_ref[0:dv, :] = v_ref[...].astype(F32).T.astype(BF16)
    vt_ref[dv:, :] = jnp.ones((vt_ref.shape[0] - dv, seq), BF16)

    lam = (jnp.exp(jnp.sum(lam_ref[0:1, :] * lam_ref[1:2, :], axis=-1, keepdims=True))
           - jnp.exp(jnp.sum(lam_ref[2:3, :] * lam_ref[3:4, :], axis=-1, keepdims=True))
           + lam_init)

    def build_qq(qi):
        q = q_ref[pl.ds(_aligned(qi * qb, qb), qb), :]
        lane = lax.broadcasted_iota(jnp.int32, q.shape, 1)
        zero = jnp.zeros_like(q)
        qq_ref[0:qb, :] = jnp.where(lane < dq, q, zero)
        qq_ref[qb:, :] = jnp.where(lane >= dq, q, zero)

    def scores(j, slot):
        k0 = _aligned(j * kb, kb)
        s = lax.dot_general(k_ref[pl.ds(k0, kb), :], qq_ref[...], (((1,), (1,)), ((), ())),
                            preferred_element_type=F32)
        s_ref[slot] = s

    def softmax(slot, masked):
        for c in range(2):
            cols = slice(c * qb, (c + 1) * qb)

            def chunk(r):
                s = s_ref[slot, r:r + row_chunk, cols]
                if masked:
                    kpos = r + lax.broadcasted_iota(jnp.int32, s.shape, 0)
                    qpos = lax.broadcasted_iota(jnp.int32, s.shape, 1)
                    s = jnp.where(kpos <= qpos, s, NEG_BIG)
                return s

            part = None
            for r in range(0, kb, row_chunk):
                t = jnp.max(chunk(r).reshape(row_chunk // SUBLANES, SUBLANES, qb), axis=0)
                part = t if part is None else jnp.maximum(part, t)
            m_old = m_ref[:, cols]
            m_new = jnp.maximum(m_old, jnp.max(part, axis=0, keepdims=True))
            al_ref[slot, :, cols] = jnp.exp2(m_old - m_new)
            m_ref[:, cols] = m_new
            for r in range(0, kb, row_chunk):
                p_ref[slot, r:r + row_chunk, cols] = jnp.exp2(chunk(r) - m_new).astype(BF16)

    def pv_update(j, slot):
        k0 = _aligned(j * kb, kb)
        pv = jnp.dot(vt_ref[:, pl.ds(k0, kb)], p_ref[slot], preferred_element_type=F32)
        acc_ref[...] = al_ref[slot] * acc_ref[...] + pv

    def finalize(qi):
        rows = pl.ds(_aligned(qi * qb, qb), qb)
        o_all = acc_ref[0:dv, :] / acc_ref[dv:dv + 1, :]
        o = o_all[:, :qb] - lam * o_all[:, qb:]
        ms = jnp.mean(o * o, axis=0, keepdims=True)
        on = (o * lax.rsqrt(ms + EPS)).T
        on = on * sg_ref[...] * (1.0 - lam_init)
        o_ref[rows, :] = (on * _silu(g_ref[rows, :].astype(F32))).astype(o_ref.dtype)

    def item(qi, j, jp, slot):
        other = 1 - slot

        def middle():
            softmax(slot, masked=False)
            pv_update(jp, other)
            scores(j + 1, other)

        def row_start():
            softmax(slot, masked=False)
            pv_update(jp, other)
            scores(1, other)
            finalize(qi - 1)
            acc_ref[...] = jnp.zeros(acc_ref.shape, F32)

        def diag():
            build_qq(jnp.minimum(qi + 1, nq - 1))
            softmax(slot, masked=True)
            pv_update(jp, other)
            scores(0, other)
            m_ref[...] = jnp.full(m_ref.shape, NEG_BIG, F32)

        is_diag = j == qi
        lax.switch(jnp.where(is_diag, 2, jnp.where(j == 0, 1, 0)), [middle, row_start, diag])
        return jnp.where(is_diag, qi + 1, qi), jnp.where(is_diag, 0, j + 1), j

    del item
    pairs = [(qi, j) for qi in range(nq) for j in range(qi + 1)]
    build_qq(0)
    m_ref[...] = jnp.full(m_ref.shape, NEG_BIG, F32)
    acc_ref[...] = jnp.zeros(acc_ref.shape, F32)
    scores(0, 0)
    for n, (qi, j) in enumerate(pairs):
        slot, prev, nxt = n % N_SLOTS, (n - 1) % N_SLOTS, (n + 1) % N_SLOTS
        if j == qi and qi + 1 < nq:
            build_qq(qi + 1)
        softmax(slot, masked=(j == qi))
        if n > 0:
            pv_update(pairs[n - 1][1], prev)
        if n + 1 < len(pairs):
            scores(pairs[n + 1][1], nxt)
        if j == 0 and qi > 0:
            finalize(qi - 1)
            acc_ref[...] = jnp.zeros(acc_ref.shape, F32)
        if j == qi:
            m_ref[...] = jnp.full(m_ref.shape, NEG_BIG, F32)
    pv_update(pairs[-1][1], (len(pairs) - 1) % N_SLOTS)
    finalize(nq - 1)


def _diff_attn(z, lam_params, subln_g, *, batch, seq, dq, dv, lam_init, qb=512):
    kb = qb
    nq = seq // qb
    assert (nq * (nq + 1) // 2) % 2 == 0, "the pair loop needs an even number of (q block, k block) pairs"
    h = N_HEADS
    k_col0 = h * 2 * dq // LANES
    v_col0 = 2 * k_col0
    g_col0 = v_col0 + h * dv // LANES
    kern = functools.partial(_attn_kernel, qb=qb, kb=kb, dq=dq, row_chunk=32, lam_init=lam_init)
    return pl.pallas_call(
        kern,
        grid=(batch, h),
        in_specs=[
            pl.BlockSpec(lam_params.shape, lambda b, hh: (0, 0)),
            pl.BlockSpec((1, dv), lambda b, hh: (0, 0)),
            pl.BlockSpec((seq, 2 * dq), lambda b, hh: (b, hh)),
            pl.BlockSpec((seq, 2 * dq), lambda b, hh: (b, k_col0 + hh)),
            pl.BlockSpec((seq, dv), lambda b, hh: (b, v_col0 + hh)),
            pl.BlockSpec((seq, dv), lambda b, hh: (b, g_col0 + hh)),
        ],
        out_specs=pl.BlockSpec((seq, dv), lambda b, hh: (b, hh)),
        out_shape=jax.ShapeDtypeStruct((batch * seq, h * dv), BF16),
        scratch_shapes=[pltpu.VMEM((dv + ONES_ROWS, seq), BF16),
                        pltpu.VMEM((2 * qb, 2 * dq), BF16),
                        pltpu.VMEM((N_SLOTS, kb, 2 * qb), F32),
                        pltpu.VMEM((N_SLOTS, kb, 2 * qb), BF16),
                        pltpu.VMEM((N_SLOTS, 1, 2 * qb), F32),
                        pltpu.VMEM((1, 2 * qb), F32),
                        pltpu.VMEM((dv + ONES_ROWS, 2 * qb), F32)],
        compiler_params=pltpu.CompilerParams(
            dimension_semantics=("arbitrary", "arbitrary"),
            vmem_limit_bytes=VMEM_LIMIT),
        name="diff_attn",
    )(lam_params, subln_g, z, z, z, z)


def _conv_kernel(ua_ref, ub_ref, ha_ref, hb_ref, gc_ref, dww_ref, dwb_ref, lng_ref, lnb_ref,
                 wpw_ref, bpw_ref, o_ref, sh_ref, wb_ref, conv_ref, *, ts, row_chunk, lane_chunk):
    i = pl.program_id(1)
    c = ua_ref.shape[1]

    @pl.when((pl.program_id(0) == 0) & (i == 0))
    def _():
        for j in range(CONV_WIDTH):
            wb_ref[j] = jnp.broadcast_to(dww_ref[j:j + 1, :], (SUBLANES, c))

    def glu(a_ref, b_ref):
        return a_ref[...].astype(F32) * jax.nn.sigmoid(b_ref[...].astype(F32))

    hist = jnp.where(i > 0, glu(ha_ref, hb_ref), 0.0)
    ypad = jnp.concatenate([hist, glu(ua_ref, ub_ref)], axis=0)
    for r in range(SUBLANES):
        n_r = HALO + ts - (SUBLANES if r else 0)
        sh_ref[r, 0:n_r, 0:c] = ypad[r:r + n_r, :]

    base = HALO - (CONV_WIDTH - 1)
    n_rc = ts // row_chunk
    for lc in range(c // lane_chunk):
        l0 = lc * lane_chunk

        def body(rc, carry, l0=l0):
            r0 = pl.multiple_of(rc * row_chunk, row_chunk)
            parts = []
            for r in range(SUBLANES):
                taps = [j for j in range(CONV_WIDTH) if (base + j) % SUBLANES == r]
                a_lo, a_hi = (base + taps[0]) // SUBLANES, (base + taps[-1]) // SUBLANES
                n_win = row_chunk + (a_hi - a_lo) * SUBLANES
                win = sh_ref[r, pl.ds(r0 + a_lo * SUBLANES, n_win), l0:l0 + lane_chunk]
                win = win.reshape(n_win // SUBLANES, SUBLANES, lane_chunk)
                part = None
                for j in taps:
                    t0 = (base + j) // SUBLANES - a_lo
                    term = win[t0:t0 + row_chunk // SUBLANES] * wb_ref[j, :, l0:l0 + lane_chunk]
                    part = term if part is None else part + term
                parts.append(part)
                if len(parts) == 3:
                    parts = [parts[0] + (parts[1] + parts[2])]
            acc = parts[0] if len(parts) == 1 else parts[0] + parts[1]
            conv_ref[pl.ds(r0, row_chunk), l0:l0 + lane_chunk] = acc.reshape(row_chunk, lane_chunk)
            return carry

        lax.fori_loop(0, n_rc, body, 0)

    y = conv_ref[...] + dwb_ref[...]
    mu = jnp.mean(y, axis=-1, keepdims=True)
    var = jnp.mean(jnp.square(y - mu), axis=-1, keepdims=True)
    y = (y - mu) * lax.rsqrt(var + EPS) * lng_ref[...] + lnb_ref[...]
    y = _silu(y)
    y = jnp.dot(y.astype(BF16), wpw_ref[...], preferred_element_type=F32) + bpw_ref[...]
    o_ref[...] = (y * _silu(gc_ref[...].astype(F32))).astype(o_ref.dtype)


def _conv_mod(z, dw_w, dw_b, ln_g, ln_b, w_pw_bf, b_pw, *, batch, seq, c, ua_col0, ts=256):
    nt = seq // ts
    ua_blk, ub_blk, gc_blk = ua_col0 // c, ua_col0 // c + 1, ua_col0 // c + 2
    halo_per_ts = ts // HALO
    kern = functools.partial(_conv_kernel, ts=ts, row_chunk=64, lane_chunk=128)

    def halo_map(blk):
        return lambda b, i: (jnp.maximum((b * nt + i) * halo_per_ts - 1, 0), blk)

    row = lambda b, i: (0, 0)
    return pl.pallas_call(
        kern,
        grid=(batch, nt),
        in_specs=[
            pl.BlockSpec((ts, c), lambda b, i: (b * nt + i, ua_blk)),
            pl.BlockSpec((ts, c), lambda b, i: (b * nt + i, ub_blk)),
            pl.BlockSpec((HALO, c), halo_map(ua_blk)),
            pl.BlockSpec((HALO, c), halo_map(ub_blk)),
            pl.BlockSpec((ts, c), lambda b, i: (b * nt + i, gc_blk)),
            pl.BlockSpec((CONV_WIDTH, c), row),
            pl.BlockSpec((1, c), row),
            pl.BlockSpec((1, c), row),
            pl.BlockSpec((1, c), row),
            pl.BlockSpec((c, c), row),
            pl.BlockSpec((1, c), row),
        ],
        out_specs=pl.BlockSpec((ts, c), lambda b, i: (b * nt + i, 0)),
        out_shape=jax.ShapeDtypeStruct((batch * seq, c), BF16),
        scratch_shapes=[pltpu.VMEM((SUBLANES, HALO + ts, c + LANES), F32),
                        pltpu.VMEM((CONV_WIDTH, SUBLANES, c), F32),
                        pltpu.VMEM((ts, c), F32)],
        compiler_params=pltpu.CompilerParams(
            dimension_semantics=("arbitrary", "arbitrary"), vmem_limit_bytes=VMEM_LIMIT),
        name="conv_mod",
    )(z, z, z, z, z, dw_w, dw_b, ln_g, ln_b, w_pw_bf, b_pw)


def _outproj_kernel(ya_ref, yc_ref, wa_ref, wc_ref, x_ref, gate_ref, fg_ref, o_ref, *, final_norm):
    mixed = jnp.dot(ya_ref[...], wa_ref[...], preferred_element_type=F32)
    mixed = mixed + jnp.dot(yc_ref[...], wc_ref[...], preferred_element_type=F32)
    xn = x_ref[...] + gate_ref[0] * mixed
    if final_norm:
        ms = jnp.mean(xn * xn, axis=-1, keepdims=True)
        xn = xn * lax.rsqrt(ms + EPS) * fg_ref[...]
    o_ref[...] = xn


def _out_proj(y_attn, y_conv, w_out_bf, x2, mod3, final_g, *, seq, final_norm, tm=512):
    m, d = x2.shape
    da, dc = y_attn.shape[1], y_conv.shape[1]
    blocks_per_seq = seq // tm
    return pl.pallas_call(
        functools.partial(_outproj_kernel, final_norm=final_norm),
        grid=(m // tm,),
        in_specs=[
            pl.BlockSpec((tm, da), lambda i: (i, 0)),
            pl.BlockSpec((tm, dc), lambda i: (i, 0)),
            pl.BlockSpec((da, d), lambda i: (0, 0)),
            pl.BlockSpec((dc, d), lambda i: (da // dc, 0)),
            pl.BlockSpec((tm, d), lambda i: (i, 0)),
            pl.BlockSpec((1, 1, d), lambda i: ((i // blocks_per_seq) * 3 + 2, 0, 0)),
            pl.BlockSpec((1, d), lambda i: (0, 0)),
        ],
        out_specs=pl.BlockSpec((tm, d), lambda i: (i, 0)),
        out_shape=jax.ShapeDtypeStruct((m, d), F32),
        compiler_params=pltpu.CompilerParams(
            dimension_semantics=("arbitrary",), vmem_limit_bytes=VMEM_LIMIT),
        name="out_proj",
    )(y_attn, y_conv, w_out_bf, w_out_bf, x2, mod3, final_g)


def kernel(x, c, positions, norm_g, w_ada, b_ada, w_in, lambda_q1, lambda_k1, lambda_q2, lambda_k2,
           subln_g, conv_dw_w, conv_dw_b, conv_ln_g, conv_ln_b, w_pw, b_pw, w_out, final_g):
    batch, seq, d = x.shape
    depth = w_in.shape[0]
    d_attn = d // 2
    d_conv = d - d_attn
    dv = d_attn // N_HEADS
    dq = dv // 2
    rot = dq // 4

    tabs = _rope_tables(positions, dq=dq, rot=rot)
    c_pad = jnp.zeros((SUBLANES, d), F32).at[:batch].set(c)

    x2 = x.reshape(batch * seq, d)
    for l in range(depth):
        lam_init = 0.8 - 0.6 * math.exp(-0.3 * l)
        mod = _ada_mod(c_pad, w_ada[l], b_ada[l].reshape(1, -1))
        mod3 = mod[:batch].reshape(batch * 3, 1, d)
        z = _in_proj(x2, norm_g[l].reshape(1, d), mod3, tabs, w_in[l].astype(BF16),
                     seq=seq, dq=dq, rot=rot, d_qk=2 * d_attn)
        lam_params = jnp.stack([lambda_q1[l], lambda_k1[l], lambda_q2[l], lambda_k2[l]]).astype(F32)
        y_attn = _diff_attn(z, lam_params, subln_g[l].reshape(1, dv),
                            batch=batch, seq=seq, dq=dq, dv=dv, lam_init=lam_init)
        y_conv = _conv_mod(z, conv_dw_w[l], conv_dw_b[l].reshape(1, -1), conv_ln_g[l].reshape(1, -1),
                           conv_ln_b[l].reshape(1, -1), w_pw[l].astype(BF16), b_pw[l].reshape(1, -1),
                           batch=batch, seq=seq, c=d_conv, ua_col0=4 * d_attn)
        x2 = _out_proj(y_attn, y_conv, w_out[l].astype(BF16), x2, mod3, final_g.reshape(1, d),
                       seq=seq, final_norm=(l == depth - 1))
    return x2.reshape(batch, seq, d)
```

```python
import functools
import math

import jax
import jax.numpy as jnp
from jax import lax
from jax.experimental import pallas as pl
from jax.experimental.pallas import tpu as pltpu

F32 = jnp.float32
BF16 = jnp.bfloat16

N_HEADS = 8
CONV_WIDTH = 31
ROPE_THETA = 500000.0
EPS = 1e-6
LANES = 128
SUBLANES = 8
HALO = 32
NEG_BIG = -1e30
ONES_ROWS = 16
LOG2E = math.log2(math.e)
N_SLOTS = 3
VMEM_LIMIT = 56 * 1024 * 1024


def _aligned(start, multiple):
    return start if isinstance(start, int) else pl.multiple_of(start, multiple)


def _silu(t):
    return t * jax.nn.sigmoid(t)


def _ada_kernel(c_ref, w_ref, b_ref, o_ref):
    ca = _silu(c_ref[...])
    o_ref[...] = jnp.dot(ca.astype(BF16), w_ref[...].astype(BF16),
                         preferred_element_type=F32) + b_ref[...]


def _ada_mod(c_pad, w_ada, b_ada, tn=1024):
    rows, d = c_pad.shape
    n = w_ada.shape[1]
    return pl.pallas_call(
        _ada_kernel,
        grid=(n // tn,),
        in_specs=[pl.BlockSpec((rows, d), lambda j: (0, 0)),
                  pl.BlockSpec((d, tn), lambda j: (0, j)),
                  pl.BlockSpec((1, tn), lambda j: (0, j))],
        out_specs=pl.BlockSpec((rows, tn), lambda j: (0, j)),
        out_shape=jax.ShapeDtypeStruct((rows, n), F32),
        compiler_params=pltpu.CompilerParams(
            dimension_semantics=("arbitrary",), vmem_limit_bytes=VMEM_LIMIT),
        name="ada_mod",
    )(c_pad, w_ada, b_ada)


def _rope_tab_kernel(pos_ref, freq_ref, cos_ref, sin_ref):
    ang = pos_ref[...].astype(F32) * freq_ref[...]
    cos_ref[...] = jnp.cos(ang)
    sin_ref[...] = jnp.sin(ang)


def _rope_tables(positions, *, dq, rot):
    n = positions.size
    nf = rot // 2
    inv_freq = ROPE_THETA ** (-jnp.arange(0, rot, 2, dtype=F32) / rot)
    freq_lane = jnp.tile(inv_freq, LANES // nf).reshape(1, LANES)
    pos_rep = jnp.repeat(positions.reshape(-1), nf).reshape(n * nf // LANES, LANES)
    cos, sin = pl.pallas_call(
        _rope_tab_kernel,
        out_shape=[jax.ShapeDtypeStruct(pos_rep.shape, F32)] * 2,
        name="rope_tab",
    )(pos_rep, freq_lane)
    cos, sin = cos.reshape(n, nf), sin.reshape(n, nf)
    one, zero = jnp.ones((n, dq - rot), F32), jnp.zeros((n, dq - rot), F32)
    z8 = jnp.zeros((n, nf), F32)
    tab_c = jnp.tile(jnp.concatenate([cos, cos, one], axis=1), (1, LANES // dq))
    tab_up = jnp.tile(jnp.concatenate([sin, z8, zero], axis=1), (1, LANES // dq))
    tab_dn = jnp.tile(jnp.concatenate([z8, sin, zero], axis=1), (1, LANES // dq))
    return tab_c, tab_up, tab_dn


def _inproj_kernel(x_ref, g_ref, scale_ref, shift_ref, tc_ref, tu_ref, td_ref, w_ref, o_ref,
                   h_ref, *, rot, q_scale, n_rope_blocks, chunk, norm_rows_per_iter):
    j = pl.program_id(1)

    @pl.when(j == 0)
    def _():
        gs = g_ref[...] * (1.0 + scale_ref[0])
        sh = shift_ref[0]

        def norm_rows(r, carry):
            rows = pl.ds(pl.multiple_of(r * norm_rows_per_iter, norm_rows_per_iter), norm_rows_per_iter)
            x = x_ref[rows, :]
            ms = jnp.mean(x * x, axis=-1, keepdims=True)
            h_ref[rows, :] = (x * lax.rsqrt(ms + EPS) * gs + sh).astype(BF16)
            return carry

        lax.fori_loop(0, x_ref.shape[0] // norm_rows_per_iter, norm_rows, 0, unroll=8)

    @pl.when(j < n_rope_blocks)
    def _():
        sc = jnp.where(j < n_rope_blocks // 2, q_scale, 1.0).astype(F32)
        cs, su, sd = tc_ref[...] * sc, tu_ref[...] * sc, td_ref[...] * sc
        for c in range(w_ref.shape[1] // chunk):
            acc = jnp.dot(h_ref[...], w_ref[:, c * chunk:(c + 1) * chunk].astype(BF16),
                          preferred_element_type=F32)
            for t in range(chunk // LANES):
                a = acc[:, t * LANES:(t + 1) * LANES]
                up = pltpu.roll(a, LANES - rot // 2, 1)
                dn = pltpu.roll(a, rot // 2, 1)
                col = c * chunk + t * LANES
                o_ref[:, col:col + LANES] = (a * cs - up * su + dn * sd).astype(o_ref.dtype)

    @pl.when(j >= n_rope_blocks)
    def _():
        o_ref[...] = jnp.dot(h_ref[...], w_ref[...].astype(BF16),
                             preferred_element_type=F32).astype(o_ref.dtype)


def _in_proj(x2, norm_g, mod3, tabs, w_in, *, seq, dq, rot, d_qk, tm=1024, tn=1024):
    m, d = x2.shape
    n = w_in.shape[1]
    blocks_per_seq = seq // tm
    kern = functools.partial(_inproj_kernel, rot=rot, q_scale=LOG2E / math.sqrt(dq),
                             n_rope_blocks=d_qk // tn, chunk=256, norm_rows_per_iter=16)
    tab_spec = pl.BlockSpec((tm, LANES), lambda i, j: (i, 0))
    return pl.pallas_call(
        kern,
        grid=(m // tm, n // tn),
        in_specs=[
            pl.BlockSpec((tm, d), lambda i, j: (i, 0)),
            pl.BlockSpec((1, d), lambda i, j: (0, 0)),
            pl.BlockSpec((1, 1, d), lambda i, j: ((i // blocks_per_seq) * 3 + 1, 0, 0)),
            pl.BlockSpec((1, 1, d), lambda i, j: ((i // blocks_per_seq) * 3 + 0, 0, 0)),
            tab_spec, tab_spec, tab_spec,
            pl.BlockSpec((d, tn), lambda i, j: (0, j)),
        ],
        out_specs=pl.BlockSpec((tm, tn), lambda i, j: (i, j)),
        out_shape=jax.ShapeDtypeStruct((m, n), BF16),
        scratch_shapes=[pltpu.VMEM((tm, d), BF16)],
        compiler_params=pltpu.CompilerParams(
            dimension_semantics=("arbitrary", "arbitrary"), vmem_limit_bytes=VMEM_LIMIT),
        name="in_proj",
    )(x2, norm_g, mod3, mod3, *tabs, w_in)


def _attn_kernel(lam_ref, sg_ref, q_ref, k_ref, v_ref, g_ref, o_ref,
                 vt_ref, qq_ref, s_ref, p_ref, al_ref, m_ref, acc_ref, *, qb, kb, dq, row_chunk, lam_init):
    seq, dv = v_ref.shape
    nq = seq // qb
    n_items = nq * (nq + 1) // 2

    vt_ref[0:dv, :] = v_ref[...].astype(F32).T.astype(BF16)
    vt_ref[dv:, :] = jnp.ones((vt_ref.shape[0] - dv, seq), BF16)

    lam = (jnp.exp(jnp.sum(lam_ref[0:1, :] * lam_ref[1:2, :], axis=-1, keepdims=True))
           - jnp.exp(jnp.sum(lam_ref[2:3, :] * lam_ref[3:4, :], axis=-1, keepdims=True))
           + lam_init)

    def build_qq(qi):
        q = q_ref[pl.ds(_aligned(qi * qb, qb), qb), :]
        lane = lax.broadcasted_iota(jnp.int32, q.shape, 1)
        zero = jnp.zeros_like(q)
        qq_ref[0:qb, :] = jnp.where(lane < dq, q, zero)
        qq_ref[qb:, :] = jnp.where(lane >= dq, q, zero)

    def scores(j, slot):
        k0 = _aligned(j * kb, kb)
        s = lax.dot_general(k_ref[pl.ds(k0, kb), :], qq_ref[...], (((1,), (1,)), ((), ())),
                            preferred_element_type=F32)
        s_ref[slot] = s

    def softmax(slot, masked):
        for c in range(2):
            cols = slice(c * qb, (c + 1) * qb)

            def chunk(r):
                s = s_ref[slot, r:r + row_chunk, cols]
                if masked:
                    kpos = r + lax.broadcasted_iota(jnp.int32, s.shape, 0)
                    qpos = lax.broadcasted_iota(jnp.int32, s.shape, 1)
                    s = jnp.where(kpos <= qpos, s, NEG_BIG)
                return s

            part = None
            for r in range(0, kb, row_chunk):
                t = jnp.max(chunk(r).reshape(row_chunk // SUBLANES, SUBLANES, qb), axis=0)
                part = t if part is None else jnp.maximum(part, t)
            m_old = m_ref[:, cols]
            m_new = jnp.maximum(m_old, jnp.max(part, axis=0, keepdims=True))
            al_ref[slot, :, cols] = jnp.exp2(m_old - m_new)
            m_ref[:, cols] = m_new
            for r in range(0, kb, row_chunk):
                p_ref[slot, r:r + row_chunk, cols] = jnp.exp2(chunk(r) - m_new).astype(BF16)

    def pv_update(j, slot):
        k0 = _aligned(j * kb, kb)
        pv = jnp.dot(vt_ref[:, pl.ds(k0, kb)], p_ref[slot], preferred_element_type=F32)
        acc_ref[...] = al_ref[slot] * acc_ref[...] + pv

    def finalize(qi):
        rows = pl.ds(_aligned(qi * qb, qb), qb)
        o_all = acc_ref[0:dv, :] / acc_ref[dv:dv + 1, :]
        o = o_all[:, :qb] - lam * o_all[:, qb:]
        ms = jnp.mean(o * o, axis=0, keepdims=True)
        on = (o * lax.rsqrt(ms + EPS)).T
        on = on * sg_ref[...] * (1.0 - lam_init)
        o_ref[rows, :] = (on * _silu(g_ref[rows, :].astype(F32))).astype(o_ref.dtype)

    pairs = [(qi, j) for qi in range(nq) for j in range(qi + 1)]
    build_qq(0)
    m_ref[...] = jnp.full(m_ref.shape, NEG_BIG, F32)
    acc_ref[...] = jnp.zeros(acc_ref.shape, F32)
    scores(0, 0)
    for n, (qi, j) in enumerate(pairs):
        slot, prev, nxt = n % N_SLOTS, (n - 1) % N_SLOTS, (n + 1) % N_SLOTS
        if j == qi and qi + 1 < nq:
            build_qq(qi + 1)
        softmax(slot, masked=(j == qi))
        if n > 0:
            pv_update(pairs[n - 1][1], prev)
        if n + 1 < len(pairs):
            scores(pairs[n + 1][1], nxt)
        if j == 0 and qi > 0:
            finalize(qi - 1)
            acc_ref[...] = jnp.zeros(acc_ref.shape, F32)
        if j == qi:
            m_ref[...] = jnp.full(m_ref.shape, NEG_BIG, F32)
    pv_update(pairs[-1][1], (len(pairs) - 1) % N_SLOTS)
    finalize(nq - 1)


def _diff_attn(z, lam_params, subln_g, *, batch, seq, dq, dv, lam_init, qb=512):
    kb = qb
    nq = seq // qb
    assert (nq * (nq + 1) // 2) % 2 == 0, "the pair loop needs an even number of (q block, k block) pairs"
    h = N_HEADS
    k_col0 = h * 2 * dq // LANES
    v_col0 = 2 * k_col0
    g_col0 = v_col0 + h * dv // LANES
    kern = functools.partial(_attn_kernel, qb=qb, kb=kb, dq=dq, row_chunk=32, lam_init=lam_init)
    return pl.pallas_call(
        kern,
        grid=(batch, h),
        in_specs=[
            pl.BlockSpec(lam_params.shape, lambda b, hh: (0, 0)),
            pl.BlockSpec((1, dv), lambda b, hh: (0, 0)),
            pl.BlockSpec((seq, 2 * dq), lambda b, hh: (b, hh)),
            pl.BlockSpec((seq, 2 * dq), lambda b, hh: (b, k_col0 + hh)),
            pl.BlockSpec((seq, dv), lambda b, hh: (b, v_col0 + hh)),
            pl.BlockSpec((seq, dv), lambda b, hh: (b, g_col0 + hh)),
        ],
        out_specs=pl.BlockSpec((seq, dv), lambda b, hh: (b, hh)),
        out_shape=jax.ShapeDtypeStruct((batch * seq, h * dv), BF16),
        scratch_shapes=[pltpu.VMEM((dv + ONES_ROWS, seq), BF16),
                        pltpu.VMEM((2 * qb, 2 * dq), BF16),
                        pltpu.VMEM((N_SLOTS, kb, 2 * qb), F32),
                        pltpu.VMEM((N_SLOTS, kb, 2 * qb), BF16),
                        pltpu.VMEM((N_SLOTS, 1, 2 * qb), F32),
                        pltpu.VMEM((1, 2 * qb), F32),
                        pltpu.VMEM((dv + ONES_ROWS, 2 * qb), F32)],
        compiler_params=pltpu.CompilerParams(
            dimension_semantics=("arbitrary", "arbitrary"),
            vmem_limit_bytes=VMEM_LIMIT),
        name="diff_attn",
    )(lam_params, subln_g, z, z, z, z)


def _conv_kernel(ua_ref, ub_ref, ha_ref, hb_ref, gc_ref, dww_ref, dwb_ref, lng_ref, lnb_ref,
                 wpw_ref, bpw_ref, o_ref, sh_ref, wb_ref, conv_ref, *, ts, row_chunk, lane_chunk):
    i = pl.program_id(1)
    c = ua_ref.shape[1]

    @pl.when((pl.program_id(0) == 0) & (i == 0))
    def _():
        for j in range(CONV_WIDTH):
            wb_ref[j] = jnp.broadcast_to(dww_ref[j:j + 1, :], (SUBLANES, c))

    def glu(a_ref, b_ref):
        return a_ref[...].astype(F32) * jax.nn.sigmoid(b_ref[...].astype(F32))

    hist = jnp.where(i > 0, glu(ha_ref, hb_ref), 0.0)
    ypad = jnp.concatenate([hist, glu(ua_ref, ub_ref)], axis=0)
    for r in range(SUBLANES):
        n_r = HALO + ts - (SUBLANES if r else 0)
        sh_ref[r, 0:n_r, 0:c] = ypad[r:r + n_r, :]

    base = HALO - (CONV_WIDTH - 1)
    n_rc = ts // row_chunk
    for lc in range(c // lane_chunk):
        l0 = lc * lane_chunk

        def body(rc, carry, l0=l0):
            r0 = pl.multiple_of(rc * row_chunk, row_chunk)
            parts = []
            for r in range(SUBLANES):
                taps = [j for j in range(CONV_WIDTH) if (base + j) % SUBLANES == r]
                a_lo, a_hi = (base + taps[0]) // SUBLANES, (base + taps[-1]) // SUBLANES
                n_win = row_chunk + (a_hi - a_lo) * SUBLANES
                win = sh_ref[r, pl.ds(r0 + a_lo * SUBLANES, n_win), l0:l0 + lane_chunk]
                win = win.reshape(n_win // SUBLANES, SUBLANES, lane_chunk)
                part = None
                for j in taps:
                    t0 = (base + j) // SUBLANES - a_lo
                    term = win[t0:t0 + row_chunk // SUBLANES] * wb_ref[j, :, l0:l0 + lane_chunk]
                    part = term if part is None else part + term
                parts.append(part)
                if len(parts) == 3:
                    parts = [parts[0] + (parts[1] + parts[2])]
            acc = parts[0] if len(parts) == 1 else parts[0] + parts[1]
            conv_ref[pl.ds(r0, row_chunk), l0:l0 + lane_chunk] = acc.reshape(row_chunk, lane_chunk)
            return carry

        lax.fori_loop(0, n_rc, body, 0)

    y = conv_ref[...] + dwb_ref[...]
    mu = jnp.mean(y, axis=-1, keepdims=True)
    var = jnp.mean(jnp.square(y - mu), axis=-1, keepdims=True)
    y = (y - mu) * lax.rsqrt(var + EPS) * lng_ref[...] + lnb_ref[...]
    y = _silu(y)
    y = jnp.dot(y.astype(BF16), wpw_ref[...], preferred_element_type=F32) + bpw_ref[...]
    o_ref[...] = (y * _silu(gc_ref[...].astype(F32))).astype(o_ref.dtype)


def _conv_mod(z, dw_w, dw_b, ln_g, ln_b, w_pw_bf, b_pw, *, batch, seq, c, ua_col0, ts=256):
    nt = seq // ts
    ua_blk, ub_blk, gc_blk = ua_col0 // c, ua_col0 // c + 1, ua_col0 // c + 2
    halo_per_ts = ts // HALO
    kern = functools.partial(_conv_kernel, ts=ts, row_chunk=64, lane_chunk=128)

    def halo_map(blk):
        return lambda b, i: (jnp.maximum((b * nt + i) * halo_per_ts - 1, 0), blk)

    row = lambda b, i: (0, 0)
    return pl.pallas_call(
        kern,
        grid=(batch, nt),
        in_specs=[
            pl.BlockSpec((ts, c), lambda b, i: (b * nt + i, ua_blk)),
            pl.BlockSpec((ts, c), lambda b, i: (b * nt + i, ub_blk)),
            pl.BlockSpec((HALO, c), halo_map(ua_blk)),
            pl.BlockSpec((HALO, c), halo_map(ub_blk)),
            pl.BlockSpec((ts, c), lambda b, i: (b * nt + i, gc_blk)),
            pl.BlockSpec((CONV_WIDTH, c), row),
            pl.BlockSpec((1, c), row),
            pl.BlockSpec((1, c), row),
            pl.BlockSpec((1, c), row),
            pl.BlockSpec((c, c), row),
            pl.BlockSpec((1, c), row),
        ],
        out_specs=pl.BlockSpec((ts, c), lambda b, i: (b * nt + i, 0)),
        out_shape=jax.ShapeDtypeStruct((batch * seq, c), BF16),
        scratch_shapes=[pltpu.VMEM((SUBLANES, HALO + ts, c + LANES), F32),
                        pltpu.VMEM((CONV_WIDTH, SUBLANES, c), F32),
                        pltpu.VMEM((ts, c), F32)],
        compiler_params=pltpu.CompilerParams(
            dimension_semantics=("arbitrary", "arbitrary"), vmem_limit_bytes=VMEM_LIMIT),
        name="conv_mod",
    )(z, z, z, z, z, dw_w, dw_b, ln_g, ln_b, w_pw_bf, b_pw)


def _outproj_kernel(ya_ref, yc_ref, wa_ref, wc_ref, x_ref, gate_ref, fg_ref, o_ref, *, final_norm):
    mixed = jnp.dot(ya_ref[...], wa_ref[...], preferred_element_type=F32)
    mixed = mixed + jnp.dot(yc_ref[...], wc_ref[...], preferred_element_type=F32)
    xn = x_ref[...] + gate_ref[0] * mixed
    if final_norm:
        ms = jnp.mean(xn * xn, axis=-1, keepdims=True)
        xn = xn * lax.rsqrt(ms + EPS) * fg_ref[...]
    o_ref[...] = xn


def _out_proj(y_attn, y_conv, w_out_bf, x2, mod3, final_g, *, seq, final_norm, tm=512):
    m, d = x2.shape
    da, dc = y_attn.shape[1], y_conv.shape[1]
    blocks_per_seq = seq // tm
    return pl.pallas_call(
        functools.partial(_outproj_kernel, final_norm=final_norm),
        grid=(m // tm,),
        in_specs=[
            pl.BlockSpec((tm, da), lambda i: (i, 0)),
            pl.BlockSpec((tm, dc), lambda i: (i, 0)),
            pl.BlockSpec((da, d), lambda i: (0, 0)),
            pl.BlockSpec((dc, d), lambda i: (da // dc, 0)),
            pl.BlockSpec((tm, d), lambda i: (i, 0)),
            pl.BlockSpec((1, 1, d), lambda i: ((i // blocks_per_seq) * 3 + 2, 0, 0)),
            pl.BlockSpec((1, d), lambda i: (0, 0)),
        ],
        out_specs=pl.BlockSpec((tm, d), lambda i: (i, 0)),
        out_shape=jax.ShapeDtypeStruct((m, d), F32),
        compiler_params=pltpu.CompilerParams(
            dimension_semantics=("arbitrary",), vmem_limit_bytes=VMEM_LIMIT),
        name="out_proj",
    )(y_attn, y_conv, w_out_bf, w_out_bf, x2, mod3, final_g)


def kernel(x, c, positions, norm_g, w_ada, b_ada, w_in, lambda_q1, lambda_k1, lambda_q2, lambda_k2,
           subln_g, conv_dw_w, conv_dw_b, conv_ln_g, conv_ln_b, w_pw, b_pw, w_out, final_g):
    batch, seq, d = x.shape
    depth = w_in.shape[0]
    d_attn = d // 2
    d_conv = d - d_attn
    dv = d_attn // N_HEADS
    dq = dv // 2
    rot = dq // 4

    tabs = _rope_tables(positions, dq=dq, rot=rot)
    c_pad = jnp.zeros((SUBLANES, d), F32).at[:batch].set(c)

    x2 = x.reshape(batch * seq, d)
    for l in range(depth):
        lam_init = 0.8 - 0.6 * math.exp(-0.3 * l)
        mod = _ada_mod(c_pad, w_ada[l], b_ada[l].reshape(1, -1))
        mod3 = mod[:batch].reshape(batch * 3, 1, d)
        z = _in_proj(x2, norm_g[l].reshape(1, d), mod3, tabs, w_in[l],
                     seq=seq, dq=dq, rot=rot, d_qk=2 * d_attn)
        lam_params = jnp.stack([lambda_q1[l], lambda_k1[l], lambda_q2[l], lambda_k2[l]]).astype(F32)
        y_attn = _diff_attn(z, lam_params, subln_g[l].reshape(1, dv),
                            batch=batch, seq=seq, dq=dq, dv=dv, lam_init=lam_init)
        y_conv = _conv_mod(z, conv_dw_w[l], conv_dw_b[l].reshape(1, -1), conv_ln_g[l].reshape(1, -1),
                           conv_ln_b[l].reshape(1, -1), w_pw[l].astype(BF16), b_pw[l].reshape(1, -1),
                           batch=batch, seq=seq, c=d_conv, ua_col0=4 * d_attn)
        x2 = _out_proj(y_attn, y_conv, w_out[l].astype(BF16), x2, mod3, final_g.reshape(1, d),
                       seq=seq, final_norm=(l == depth - 1))
    return x2.reshape(batch, seq, d)
```

```python
import functools
import math

import jax
import jax.numpy as jnp
from jax import lax
from jax.experimental import pallas as pl
from jax.experimental.pallas import tpu as pltpu

F32 = jnp.float32
BF16 = jnp.bfloat16

N_HEADS = 8
CONV_WIDTH = 31
ROPE_THETA = 500000.0
EPS = 1e-6
LANES = 128
SUBLANES = 8
HALO = 32
NEG_BIG = -1e30
ONES_ROWS = 16
LOG2E = math.log2(math.e)
N_SLOTS = 3
VMEM_LIMIT = 56 * 1024 * 1024


def _aligned(start, multiple):
    return start if isinstance(start, int) else pl.multiple_of(start, multiple)


def _silu(t):
    return t * jax.nn.sigmoid(t)


def _ada_kernel(c_ref, w_ref, b_ref, o_ref):
    ca = _silu(c_ref[...])
    o_ref[...] = jnp.dot(ca.astype(BF16), w_ref[...].astype(BF16),
                         preferred_element_type=F32) + b_ref[...]


def _ada_mod(c_pad, w_ada, b_ada, tn=1024):
    rows, d = c_pad.shape
    n = w_ada.shape[1]
    return pl.pallas_call(
        _ada_kernel,
        grid=(n // tn,),
        in_specs=[pl.BlockSpec((rows, d), lambda j: (0, 0)),
                  pl.BlockSpec((d, tn), lambda j: (0, j)),
                  pl.BlockSpec((1, tn), lambda j: (0, j))],
        out_specs=pl.BlockSpec((rows, tn), lambda j: (0, j)),
        out_shape=jax.ShapeDtypeStruct((rows, n), F32),
        compiler_params=pltpu.CompilerParams(
            dimension_semantics=("arbitrary",), vmem_limit_bytes=VMEM_LIMIT),
        name="ada_mod",
    )(c_pad, w_ada, b_ada)


def _rope_tab_kernel(pos_ref, freq_ref, cos_ref, sin_ref):
    ang = pos_ref[...].astype(F32) * freq_ref[...]
    cos_ref[...] = jnp.cos(ang)
    sin_ref[...] = jnp.sin(ang)


def _rope_tables(positions, *, dq, rot):
    n = positions.size
    nf = rot // 2
    inv_freq = ROPE_THETA ** (-jnp.arange(0, rot, 2, dtype=F32) / rot)
    freq_lane = jnp.tile(inv_freq, LANES // nf).reshape(1, LANES)
    pos_rep = jnp.repeat(positions.reshape(-1), nf).reshape(n * nf // LANES, LANES)
    cos, sin = pl.pallas_call(
        _rope_tab_kernel,
        out_shape=[jax.ShapeDtypeStruct(pos_rep.shape, F32)] * 2,
        name="rope_tab",
    )(pos_rep, freq_lane)
    cos, sin = cos.reshape(n, nf), sin.reshape(n, nf)
    one, zero = jnp.ones((n, dq - rot), F32), jnp.zeros((n, dq - rot), F32)
    z8 = jnp.zeros((n, nf), F32)
    tab_c = jnp.tile(jnp.concatenate([cos, cos, one], axis=1), (1, LANES // dq))
    tab_up = jnp.tile(jnp.concatenate([sin, z8, zero], axis=1), (1, LANES // dq))
    tab_dn = jnp.tile(jnp.concatenate([z8, sin, zero], axis=1), (1, LANES // dq))
    return tab_c, tab_up, tab_dn


def _inproj_kernel(x_ref, g_ref, scale_ref, shift_ref, tc_ref, tu_ref, td_ref, w_ref, o_ref,
                   h_ref, *, rot, q_scale, n_rope_blocks, chunk, norm_rows_per_iter):
    j = pl.program_id(1)

    @pl.when(j == 0)
    def _():
        gs = g_ref[...] * (1.0 + scale_ref[0])
        sh = shift_ref[0]

        def norm_rows(r, carry):
            rows = pl.ds(pl.multiple_of(r * norm_rows_per_iter, norm_rows_per_iter), norm_rows_per_iter)
            x = x_ref[rows, :]
            ms = jnp.mean(x * x, axis=-1, keepdims=True)
            h_ref[rows, :] = (x * lax.rsqrt(ms + EPS) * gs + sh).astype(BF16)
            return carry

        lax.fori_loop(0, x_ref.shape[0] // norm_rows_per_iter, norm_rows, 0, unroll=8)

    @pl.when(j < n_rope_blocks)
    def _():
        sc = jnp.where(j < n_rope_blocks // 2, q_scale, 1.0).astype(F32)
        cs, su, sd = tc_ref[...] * sc, tu_ref[...] * sc, td_ref[...] * sc
        for c in range(w_ref.shape[1] // chunk):
            acc = jnp.dot(h_ref[...], w_ref[:, c * chunk:(c + 1) * chunk].astype(BF16),
                          preferred_element_type=F32)
            for t in range(chunk // LANES):
                a = acc[:, t * LANES:(t + 1) * LANES]
                up = pltpu.roll(a, LANES - rot // 2, 1)
                dn = pltpu.roll(a, rot // 2, 1)
                col = c * chunk + t * LANES
                o_ref[:, col:col + LANES] = (a * cs - up * su + dn * sd).astype(o_ref.dtype)

    @pl.when(j >= n_rope_blocks)
    def _():
        o_ref[...] = jnp.dot(h_ref[...], w_ref[...].astype(BF16),
                             preferred_element_type=F32).astype(o_ref.dtype)


def _in_proj(x2, norm_g, mod3, tabs, w_in, *, seq, dq, rot, d_qk, tm=1024, tn=1024):
    m, d = x2.shape
    n = w_in.shape[1]
    blocks_per_seq = seq // tm
    kern = functools.partial(_inproj_kernel, rot=rot, q_scale=LOG2E / math.sqrt(dq),
                             n_rope_blocks=d_qk // tn, chunk=256, norm_rows_per_iter=16)
    tab_spec = pl.BlockSpec((tm, LANES), lambda i, j: (i, 0))
    return pl.pallas_call(
        kern,
        grid=(m // tm, n // tn),
        in_specs=[
            pl.BlockSpec((tm, d), lambda i, j: (i, 0)),
            pl.BlockSpec((1, d), lambda i, j: (0, 0)),
            pl.BlockSpec((1, 1, d), lambda i, j: ((i // blocks_per_seq) * 3 + 1, 0, 0)),
            pl.BlockSpec((1, 1, d), lambda i, j: ((i // blocks_per_seq) * 3 + 0, 0, 0)),
            tab_spec, tab_spec, tab_spec,
            pl.BlockSpec((d, tn), lambda i, j: (0, j)),
        ],
        out_specs=pl.BlockSpec((tm, tn), lambda i, j: (i, j)),
        out_shape=jax.ShapeDtypeStruct((m, n), BF16),
        scratch_shapes=[pltpu.VMEM((tm, d), BF16)],
        compiler_params=pltpu.CompilerParams(
            dimension_semantics=("arbitrary", "arbitrary"), vmem_limit_bytes=VMEM_LIMIT),
        name="in_proj",
    )(x2, norm_g, mod3, mod3, *tabs, w_in)


def _attn_kernel(lam_ref, sg_ref, q_ref, k_ref, v_ref, g_ref, o_ref,
                 vt_ref, qq_ref, s_ref, p_ref, al_ref, m_ref, acc_ref, *, qb, kb, dq, row_chunk, lam_init):
    seq, dv = v_ref.shape
    nq = seq // qb
    n_items = nq * (nq + 1) // 2

    vt_ref[0:dv, :] = v_ref[...].astype(F32).T.astype(BF16)
    vt_ref[dv:, :] = jnp.ones((vt_ref.shape[0] - dv, seq), BF16)

    lam = (jnp.exp(jnp.sum(lam_ref[0:1, :] * lam_ref[1:2, :], axis=-1, keepdims=True))
           - jnp.exp(jnp.sum(lam_ref[2:3, :] * lam_ref[3:4, :], axis=-1, keepdims=True))
           + lam_init)

    def build_qq(qi):
        q = q_ref[pl.ds(_aligned(qi * qb, qb), qb), :]
        lane = lax.broadcasted_iota(jnp.int32, q.shape, 1)
        zero = jnp.zeros_like(q)
        qq_ref[0:qb, :] = jnp.where(lane < dq, q, zero)
        qq_ref[qb:, :] = jnp.where(lane >= dq, q, zero)

    def scores(j, slot):
        k0 = _aligned(j * kb, kb)
        s = lax.dot_general(k_ref[pl.ds(k0, kb), :], qq_ref[...], (((1,), (1,)), ((), ())),
                            preferred_element_type=F32)
        s_ref[slot] = s

    def softmax(slot, masked):
        for c in range(2):
            cols = slice(c * qb, (c + 1) * qb)

            def chunk(r):
                s = s_ref[slot, r:r + row_chunk, cols]
                if masked:
                    kpos = r + lax.broadcasted_iota(jnp.int32, s.shape, 0)
                    qpos = lax.broadcasted_iota(jnp.int32, s.shape, 1)
                    s = jnp.where(kpos <= qpos, s, NEG_BIG)
                return s

            part = None
            for r in range(0, kb, row_chunk):
                t = jnp.max(chunk(r).reshape(row_chunk // SUBLANES, SUBLANES, qb), axis=0)
                part = t if part is None else jnp.maximum(part, t)
            m_old = m_ref[:, cols]
            m_new = jnp.maximum(m_old, jnp.max(part, axis=0, keepdims=True))
            al_ref[slot, :, cols] = jnp.exp2(m_old - m_new)
            m_ref[:, cols] = m_new
            for r in range(0, kb, row_chunk):
                p_ref[slot, r:r + row_chunk, cols] = jnp.exp2(chunk(r) - m_new).astype(BF16)

    def pv_update(j, slot):
        k0 = _aligned(j * kb, kb)
        pv = jnp.dot(vt_ref[:, pl.ds(k0, kb)], p_ref[slot], preferred_element_type=F32)
        acc_ref[...] = al_ref[slot] * acc_ref[...] + pv

    def finalize(qi):
        rows = pl.ds(_aligned(qi * qb, qb), qb)
        o_all = acc_ref[0:dv, :] / acc_ref[dv:dv + 1, :]
        o = o_all[:, :qb] - lam * o_all[:, qb:]
        ms = jnp.mean(o * o, axis=0, keepdims=True)
        on = (o * lax.rsqrt(ms + EPS)).T
        on = on * sg_ref[...] * (1.0 - lam_init)
        o_ref[rows, :] = (on * _silu(g_ref[rows, :].astype(F32))).astype(o_ref.dtype)

    pairs = [(qi, j) for qi in range(nq) for j in range(qi + 1)]
    build_qq(0)
    m_ref[...] = jnp.full(m_ref.shape, NEG_BIG, F32)
    acc_ref[...] = jnp.zeros(acc_ref.shape, F32)
    scores(0, 0)
    for n, (qi, j) in enumerate(pairs):
        slot, prev, nxt = n % N_SLOTS, (n - 1) % N_SLOTS, (n + 1) % N_SLOTS
        if j == qi and qi + 1 < nq:
            build_qq(qi + 1)
        softmax(slot, masked=(j == qi))
        if n > 0:
            pv_update(pairs[n - 1][1], prev)
        if n + 1 < len(pairs):
            scores(pairs[n + 1][1], nxt)
        if j == 0 and qi > 0:
            finalize(qi - 1)
            acc_ref[...] = jnp.zeros(acc_ref.shape, F32)
        if j == qi:
            m_ref[...] = jnp.full(m_ref.shape, NEG_BIG, F32)
    pv_update(pairs[-1][1], (len(pairs) - 1) % N_SLOTS)
    finalize(nq - 1)


def _diff_attn(z, lam_params, subln_g, *, batch, seq, dq, dv, lam_init, qb=512):
    kb = qb
    nq = seq // qb
    assert (nq * (nq + 1) // 2) % 2 == 0, "the pair loop needs an even number of (q block, k block) pairs"
    h = N_HEADS
    k_col0 = h * 2 * dq // LANES
    v_col0 = 2 * k_col0
    g_col0 = v_col0 + h * dv // LANES
    kern = functools.partial(_attn_kernel, qb=qb, kb=kb, dq=dq, row_chunk=32, lam_init=lam_init)
    return pl.pallas_call(
        kern,
        grid=(batch, h),
        in_specs=[
            pl.BlockSpec(lam_params.shape, lambda b, hh: (0, 0)),
            pl.BlockSpec((1, dv), lambda b, hh: (0, 0)),
            pl.BlockSpec((seq, 2 * dq), lambda b, hh: (b, hh)),
            pl.BlockSpec((seq, 2 * dq), lambda b, hh: (b, k_col0 + hh)),
            pl.BlockSpec((seq, dv), lambda b, hh: (b, v_col0 + hh)),
            pl.BlockSpec((seq, dv), lambda b, hh: (b, g_col0 + hh)),
        ],
        out_specs=pl.BlockSpec((seq, dv), lambda b, hh: (b, hh)),
        out_shape=jax.ShapeDtypeStruct((batch * seq, h * dv), BF16),
        scratch_shapes=[pltpu.VMEM((dv + ONES_ROWS, seq), BF16),
                        pltpu.VMEM((2 * qb, 2 * dq), BF16),
                        pltpu.VMEM((N_SLOTS, kb, 2 * qb), F32),
                        pltpu.VMEM((N_SLOTS, kb, 2 * qb), BF16),
                        pltpu.VMEM((N_SLOTS, 1, 2 * qb), F32),
                        pltpu.VMEM((1, 2 * qb), F32),
                        pltpu.VMEM((dv + ONES_ROWS, 2 * qb), F32)],
        compiler_params=pltpu.CompilerParams(
            dimension_semantics=("arbitrary", "arbitrary"),
            vmem_limit_bytes=VMEM_LIMIT),
        name="diff_attn",
    )(lam_params, subln_g, z, z, z, z)


def _conv_kernel(ua_ref, ub_ref, ha_ref, hb_ref, gc_ref, dww_ref, dwb_ref, lng_ref, lnb_ref,
                 wpw_ref, bpw_ref, o_ref, sh_ref, wb_ref, conv_ref, *, ts, row_chunk, lane_chunk):
    i = pl.program_id(1)
    c = ua_ref.shape[1]

    @pl.when((pl.program_id(0) == 0) & (i == 0))
    def _():
        for j in range(CONV_WIDTH):
            wb_ref[j] = jnp.broadcast_to(dww_ref[j:j + 1, :], (SUBLANES, c))

    def glu(a_ref, b_ref):
        return a_ref[...].astype(F32) * jax.nn.sigmoid(b_ref[...].astype(F32))

    hist = jnp.where(i > 0, glu(ha_ref, hb_ref), 0.0)
    ypad = jnp.concatenate([hist, glu(ua_ref, ub_ref)], axis=0)
    for r in range(SUBLANES):
        n_r = HALO + ts - (SUBLANES if r else 0)
        sh_ref[r, 0:n_r, 0:c] = ypad[r:r + n_r, :]

    base = HALO - (CONV_WIDTH - 1)
    n_rc = ts // row_chunk
    for lc in range(c // lane_chunk):
        l0 = lc * lane_chunk

        def body(rc, carry, l0=l0):
            r0 = pl.multiple_of(rc * row_chunk, row_chunk)
            parts = []
            for r in range(SUBLANES):
                taps = [j for j in range(CONV_WIDTH) if (base + j) % SUBLANES == r]
                a_lo, a_hi = (base + taps[0]) // SUBLANES, (base + taps[-1]) // SUBLANES
                n_win = row_chunk + (a_hi - a_lo) * SUBLANES
                win = sh_ref[r, pl.ds(r0 + a_lo * SUBLANES, n_win), l0:l0 + lane_chunk]
                win = win.reshape(n_win // SUBLANES, SUBLANES, lane_chunk)
                part = None
                for j in taps:
                    t0 = (base + j) // SUBLANES - a_lo
                    term = win[t0:t0 + row_chunk // SUBLANES] * wb_ref[j, :, l0:l0 + lane_chunk]
                    part = term if part is None else part + term
                parts.append(part)
                if len(parts) == 3:
                    parts = [parts[0] + (parts[1] + parts[2])]
            acc = parts[0] if len(parts) == 1 else parts[0] + parts[1]
            conv_ref[pl.ds(r0, row_chunk), l0:l0 + lane_chunk] = acc.reshape(row_chunk, lane_chunk)
            return carry

        lax.fori_loop(0, n_rc, body, 0)

    y = conv_ref[...] + dwb_ref[...]
    mu = jnp.mean(y, axis=-1, keepdims=True)
    var = jnp.mean(jnp.square(y - mu), axis=-1, keepdims=True)
    y = (y - mu) * lax.rsqrt(var + EPS) * lng_ref[...] + lnb_ref[...]
    y = _silu(y)
    y = jnp.dot(y.astype(BF16), wpw_ref[...], preferred_element_type=F32) + bpw_ref[...]
    o_ref[...] = (y * _silu(gc_ref[...].astype(F32))).astype(o_ref.dtype)


def _conv_mod(z, dw_w, dw_b, ln_g, ln_b, w_pw_bf, b_pw, *, batch, seq, c, ua_col0, ts=512):
    nt = seq // ts
    ua_blk, ub_blk, gc_blk = ua_col0 // c, ua_col0 // c + 1, ua_col0 // c + 2
    halo_per_ts = ts // HALO
    kern = functools.partial(_conv_kernel, ts=ts, row_chunk=64, lane_chunk=128)

    def halo_map(blk):
        return lambda b, i: (jnp.maximum((b * nt + i) * halo_per_ts - 1, 0), blk)

    row = lambda b, i: (0, 0)
    return pl.pallas_call(
        kern,
        grid=(batch, nt),
        in_specs=[
            pl.BlockSpec((ts, c), lambda b, i: (b * nt + i, ua_blk)),
            pl.BlockSpec((ts, c), lambda b, i: (b * nt + i, ub_blk)),
            pl.BlockSpec((HALO, c), halo_map(ua_blk)),
            pl.BlockSpec((HALO, c), halo_map(ub_blk)),
            pl.BlockSpec((ts, c), lambda b, i: (b * nt + i, gc_blk)),
            pl.BlockSpec((CONV_WIDTH, c), row),
            pl.BlockSpec((1, c), row),
            pl.BlockSpec((1, c), row),
            pl.BlockSpec((1, c), row),
            pl.BlockSpec((c, c), row),
            pl.BlockSpec((1, c), row),
        ],
        out_specs=pl.BlockSpec((ts, c), lambda b, i: (b * nt + i, 0)),
        out_shape=jax.ShapeDtypeStruct((batch * seq, c), BF16),
        scratch_shapes=[pltpu.VMEM((SUBLANES, HALO + ts, c + LANES), F32),
                        pltpu.VMEM((CONV_WIDTH, SUBLANES, c), F32),
                        pltpu.VMEM((ts, c), F32)],
        compiler_params=pltpu.CompilerParams(
            dimension_semantics=("arbitrary", "arbitrary"), vmem_limit_bytes=VMEM_LIMIT),
        name="conv_mod",
    )(z, z, z, z, z, dw_w, dw_b, ln_g, ln_b, w_pw_bf, b_pw)


def _outproj_kernel(ya_ref, yc_ref, wa_ref, wc_ref, x_ref, gate_ref, fg_ref, o_ref, *, final_norm):
    mixed = jnp.dot(ya_ref[...], wa_ref[...].astype(BF16), preferred_element_type=F32)
    mixed = mixed + jnp.dot(yc_ref[...], wc_ref[...].astype(BF16), preferred_element_type=F32)
    xn = x_ref[...] + gate_ref[0] * mixed
    if final_norm:
        ms = jnp.mean(xn * xn, axis=-1, keepdims=True)
        xn = xn * lax.rsqrt(ms + EPS) * fg_ref[...]
    o_ref[...] = xn


def _out_proj(y_attn, y_conv, w_out, x2, mod3, final_g, *, seq, final_norm, tm=512):
    m, d = x2.shape
    da, dc = y_attn.shape[1], y_conv.shape[1]
    blocks_per_seq = seq // tm
    return pl.pallas_call(
        functools.partial(_outproj_kernel, final_norm=final_norm),
        grid=(m // tm,),
        in_specs=[
            pl.BlockSpec((tm, da), lambda i: (i, 0)),
            pl.BlockSpec((tm, dc), lambda i: (i, 0)),
            pl.BlockSpec((da, d), lambda i: (0, 0)),
            pl.BlockSpec((dc, d), lambda i: (da // dc, 0)),
            pl.BlockSpec((tm, d), lambda i: (i, 0)),
            pl.BlockSpec((1, 1, d), lambda i: ((i // blocks_per_seq) * 3 + 2, 0, 0)),
            pl.BlockSpec((1, d), lambda i: (0, 0)),
        ],
        out_specs=pl.BlockSpec((tm, d), lambda i: (i, 0)),
        out_shape=jax.ShapeDtypeStruct((m, d), F32),
        compiler_params=pltpu.CompilerParams(
            dimension_semantics=("arbitrary",), vmem_limit_bytes=VMEM_LIMIT),
        name="out_proj",
    )(y_attn, y_conv, w_out, w_out, x2, mod3, final_g)


def kernel(x, c, positions, norm_g, w_ada, b_ada, w_in, lambda_q1, lambda_k1, lambda_q2, lambda_k2,
           subln_g, conv_dw_w, conv_dw_b, conv_ln_g, conv_ln_b, w_pw, b_pw, w_out, final_g):
    batch, seq, d = x.shape
    depth = w_in.shape[0]
    d_attn = d // 2
    d_conv = d - d_attn
    dv = d_attn // N_HEADS
    dq = dv // 2
    rot = dq // 4

    tabs = _rope_tables(positions, dq=dq, rot=rot)
    c_pad = jnp.zeros((SUBLANES, d), F32).at[:batch].set(c)

    x2 = x.reshape(batch * seq, d)
    for l in range(depth):
        lam_init = 0.8 - 0.6 * math.exp(-0.3 * l)
        mod = _ada_mod(c_pad, w_ada[l], b_ada[l].reshape(1, -1))
        mod3 = mod[:batch].reshape(batch * 3, 1, d)
        z = _in_proj(x2, norm_g[l].reshape(1, d), mod3, tabs, w_in[l],
                     seq=seq, dq=dq, rot=rot, d_qk=2 * d_attn)
        lam_params = jnp.stack([lambda_q1[l], lambda_k1[l], lambda_q2[l], lambda_k2[l]]).astype(F32)
        y_attn = _diff_attn(z, lam_params, subln_g[l].reshape(1, dv),
                            batch=batch, seq=seq, dq=dq, dv=dv, lam_init=lam_init)
        y_conv = _conv_mod(z, conv_dw_w[l], conv_dw_b[l].reshape(1, -1), conv_ln_g[l].reshape(1, -1),
                           conv_ln_b[l].reshape(1, -1), w_pw[l].astype(BF16), b_pw[l].reshape(1, -1),
                           batch=batch, seq=seq, c=d_conv, ua_col0=4 * d_attn)
        x2 = _out_proj(y_attn, y_conv, w_out[l], x2, mod3, final_g.reshape(1, d),
                       seq=seq, final_norm=(l == depth - 1))
    return x2.reshape(batch, seq, d)
```

```python
import functools
import math

import jax
import jax.numpy as jnp
from jax import lax
from jax.experimental import pallas as pl
from jax.experimental.pallas import tpu as pltpu

F32 = jnp.float32
BF16 = jnp.bfloat16

N_HEADS = 8
CONV_WIDTH = 31
ROPE_THETA = 500000.0
EPS = 1e-6
LANES = 128
SUBLANES = 8
HALO = 32
NEG_BIG = -1e30
ONES_ROWS = 16
LOG2E = math.log2(math.e)
N_SLOTS = 3
VMEM_LIMIT = 56 * 1024 * 1024


def _aligned(start, multiple):
    return start if isinstance(start, int) else pl.multiple_of(start, multiple)


def _silu(t):
    return t * jax.nn.sigmoid(t)


def _ada_kernel(c_ref, w_ref, b_ref, o_ref):
    ca = _silu(c_ref[...])
    o_ref[...] = jnp.dot(ca.astype(BF16), w_ref[...].astype(BF16),
                         preferred_element_type=F32) + b_ref[...]


def _ada_mod(c_pad, w_ada, b_ada, tn=1024):
    rows, d = c_pad.shape
    n = w_ada.shape[1]
    return pl.pallas_call(
        _ada_kernel,
        grid=(n // tn,),
        in_specs=[pl.BlockSpec((rows, d), lambda j: (0, 0)),
                  pl.BlockSpec((d, tn), lambda j: (0, j)),
                  pl.BlockSpec((1, tn), lambda j: (0, j))],
        out_specs=pl.BlockSpec((rows, tn), lambda j: (0, j)),
        out_shape=jax.ShapeDtypeStruct((rows, n), F32),
        compiler_params=pltpu.CompilerParams(
            dimension_semantics=("arbitrary",), vmem_limit_bytes=VMEM_LIMIT),
        name="ada_mod",
    )(c_pad, w_ada, b_ada)


def _rope_tab_kernel(pos_ref, freq_ref, cos_ref, sin_ref):
    ang = pos_ref[...].astype(F32) * freq_ref[...]
    cos_ref[...] = jnp.cos(ang)
    sin_ref[...] = jnp.sin(ang)


def _rope_tables(positions, *, dq, rot):
    n = positions.size
    nf = rot // 2
    inv_freq = ROPE_THETA ** (-jnp.arange(0, rot, 2, dtype=F32) / rot)
    freq_lane = jnp.tile(inv_freq, LANES // nf).reshape(1, LANES)
    pos_rep = jnp.repeat(positions.reshape(-1), nf).reshape(n * nf // LANES, LANES)
    cos, sin = pl.pallas_call(
        _rope_tab_kernel,
        out_shape=[jax.ShapeDtypeStruct(pos_rep.shape, F32)] * 2,
        name="rope_tab",
    )(pos_rep, freq_lane)
    cos, sin = cos.reshape(n, nf), sin.reshape(n, nf)
    one, zero = jnp.ones((n, dq - rot), F32), jnp.zeros((n, dq - rot), F32)
    tab_c = jnp.tile(jnp.concatenate([cos, cos, one], axis=1), (1, LANES // dq))
    tab_s = jnp.tile(jnp.concatenate([sin, sin, zero], axis=1), (1, LANES // dq))
    return tab_c, tab_s


def _inproj_kernel(x_ref, g_ref, scale_ref, shift_ref, tc_ref, ts_ref, w_ref, o_ref,
                   h_ref, *, dq, rot, q_scale, n_rope_blocks, chunk, norm_rows_per_iter):
    j = pl.program_id(1)

    @pl.when(j == 0)
    def _():
        gs = g_ref[...] * (1.0 + scale_ref[0])
        sh = shift_ref[0]

        def norm_rows(r, carry):
            rows = pl.ds(pl.multiple_of(r * norm_rows_per_iter, norm_rows_per_iter), norm_rows_per_iter)
            x = x_ref[rows, :]
            ms = jnp.mean(x * x, axis=-1, keepdims=True)
            h_ref[rows, :] = (x * lax.rsqrt(ms + EPS) * gs + sh).astype(BF16)
            return carry

        lax.fori_loop(0, x_ref.shape[0] // norm_rows_per_iter, norm_rows, 0, unroll=8)

    @pl.when(j < n_rope_blocks)
    def _():
        sc = jnp.where(j < n_rope_blocks // 2, q_scale, 1.0).astype(F32)
        cs, sn = tc_ref[...] * sc, ts_ref[...] * sc
        d = lax.broadcasted_iota(jnp.int32, sn.shape, 1) % dq
        su = jnp.where(d < rot // 2, sn, 0.0)
        sd = jnp.where(d >= rot // 2, sn, 0.0)
        for c in range(w_ref.shape[1] // chunk):
            acc = jnp.dot(h_ref[...], w_ref[:, c * chunk:(c + 1) * chunk].astype(BF16),
                          preferred_element_type=F32)
            for t in range(chunk // LANES):
                a = acc[:, t * LANES:(t + 1) * LANES]
                up = pltpu.roll(a, LANES - rot // 2, 1)
                dn = pltpu.roll(a, rot // 2, 1)
                col = c * chunk + t * LANES
                o_ref[:, col:col + LANES] = (a * cs - up * su + dn * sd).astype(o_ref.dtype)

    @pl.when(j >= n_rope_blocks)
    def _():
        o_ref[...] = jnp.dot(h_ref[...], w_ref[...].astype(BF16),
                             preferred_element_type=F32).astype(o_ref.dtype)


def _in_proj(x2, norm_g, mod3, tabs, w_in, *, seq, dq, rot, d_qk, tm=1024, tn=1024):
    m, d = x2.shape
    n = w_in.shape[1]
    blocks_per_seq = seq // tm
    kern = functools.partial(_inproj_kernel, dq=dq, rot=rot, q_scale=LOG2E / math.sqrt(dq),
                             n_rope_blocks=d_qk // tn, chunk=256, norm_rows_per_iter=16)
    tab_spec = pl.BlockSpec((tm, LANES), lambda i, j: (i, 0))
    return pl.pallas_call(
        kern,
        grid=(m // tm, n // tn),
        in_specs=[
            pl.BlockSpec((tm, d), lambda i, j: (i, 0)),
            pl.BlockSpec((1, d), lambda i, j: (0, 0)),
            pl.BlockSpec((1, 1, d), lambda i, j: ((i // blocks_per_seq) * 3 + 1, 0, 0)),
            pl.BlockSpec((1, 1, d), lambda i, j: ((i // blocks_per_seq) * 3 + 0, 0, 0)),
            tab_spec, tab_spec,
            pl.BlockSpec((d, tn), lambda i, j: (0, j)),
        ],
        out_specs=pl.BlockSpec((tm, tn), lambda i, j: (i, j)),
        out_shape=jax.ShapeDtypeStruct((m, n), BF16),
        scratch_shapes=[pltpu.VMEM((tm, d), BF16)],
        compiler_params=pltpu.CompilerParams(
            dimension_semantics=("arbitrary", "arbitrary"), vmem_limit_bytes=VMEM_LIMIT),
        name="in_proj",
    )(x2, norm_g, mod3, mod3, *tabs, w_in)


def _attn_kernel(lam_ref, sg_ref, q_ref, k_ref, v_ref, g_ref, o_ref,
                 vt_ref, qq_ref, s_ref, p_ref, al_ref, m_ref, acc_ref, *, qb, kb, dq, row_chunk, lam_init):
    seq, dv = v_ref.shape
    nq = seq // qb
    n_items = nq * (nq + 1) // 2

    vt_ref[0:dv, :] = v_ref[...].astype(F32).T.astype(BF16)
    vt_ref[dv:, :] = jnp.ones((vt_ref.shape[0] - dv, seq), BF16)

    lam = (jnp.exp(jnp.sum(lam_ref[0:1, :] * lam_ref[1:2, :], axis=-1, keepdims=True))
           - jnp.exp(jnp.sum(lam_ref[2:3, :] * lam_ref[3:4, :], axis=-1, keepdims=True))
           + lam_init)

    def build_qq(qi):
        q = q_ref[pl.ds(_aligned(qi * qb, qb), qb), :]
        lane = lax.broadcasted_iota(jnp.int32, q.shape, 1)
        zero = jnp.zeros_like(q)
        qq_ref[0:qb, :] = jnp.where(lane < dq, q, zero)
        qq_ref[qb:, :] = jnp.where(lane >= dq, q, zero)

    def scores(j, slot):
        k0 = _aligned(j * kb, kb)
        s = lax.dot_general(k_ref[pl.ds(k0, kb), :], qq_ref[...], (((1,), (1,)), ((), ())),
                            preferred_element_type=F32)
        s_ref[slot] = s

    def softmax(slot, masked):
        for c in range(2):
            cols = slice(c * qb, (c + 1) * qb)

            def chunk(r):
                s = s_ref[slot, r:r + row_chunk, cols]
                if masked:
                    kpos = r + lax.broadcasted_iota(jnp.int32, s.shape, 0)
                    qpos = lax.broadcasted_iota(jnp.int32, s.shape, 1)
                    s = jnp.where(kpos <= qpos, s, NEG_BIG)
                return s

            part = None
            for r in range(0, kb, row_chunk):
                t = jnp.max(chunk(r).reshape(row_chunk // SUBLANES, SUBLANES, qb), axis=0)
                part = t if part is None else jnp.maximum(part, t)
            m_old = m_ref[:, cols]
            m_new = jnp.maximum(m_old, jnp.max(part, axis=0, keepdims=True))
            al_ref[slot, :, cols] = jnp.exp2(m_old - m_new)
            m_ref[:, cols] = m_new
            for r in range(0, kb, row_chunk):
                p_ref[slot, r:r + row_chunk, cols] = jnp.exp2(chunk(r) - m_new).astype(BF16)

    def pv_update(j, slot):
        k0 = _aligned(j * kb, kb)
        pv = jnp.dot(vt_ref[:, pl.ds(k0, kb)], p_ref[slot], preferred_element_type=F32)
        acc_ref[...] = al_ref[slot] * acc_ref[...] + pv

    def finalize(qi):
        rows = pl.ds(_aligned(qi * qb, qb), qb)
        o_all = acc_ref[0:dv, :] / acc_ref[dv:dv + 1, :]
        o = o_all[:, :qb] - lam * o_all[:, qb:]
        ms = jnp.mean(o * o, axis=0, keepdims=True)
        on = (o * lax.rsqrt(ms + EPS)).T
        on = on * sg_ref[...] * (1.0 - lam_init)
        o_ref[rows, :] = (on * _silu(g_ref[rows, :].astype(F32))).astype(o_ref.dtype)

    pairs = [(qi, j) for qi in range(nq) for j in range(qi + 1)]
    build_qq(0)
    m_ref[...] = jnp.full(m_ref.shape, NEG_BIG, F32)
    acc_ref[...] = jnp.zeros(acc_ref.shape, F32)
    scores(0, 0)
    for n, (qi, j) in enumerate(pairs):
        slot, prev, nxt = n % N_SLOTS, (n - 1) % N_SLOTS, (n + 1) % N_SLOTS
        if j == qi and qi + 1 < nq:
            build_qq(qi + 1)
        softmax(slot, masked=(j == qi))
        if n > 0:
            pv_update(pairs[n - 1][1], prev)
        if n + 1 < len(pairs):
            scores(pairs[n + 1][1], nxt)
        if j == 0 and qi > 0:
            finalize(qi - 1)
            acc_ref[...] = jnp.zeros(acc_ref.shape, F32)
        if j == qi:
            m_ref[...] = jnp.full(m_ref.shape, NEG_BIG, F32)
    pv_update(pairs[-1][1], (len(pairs) - 1) % N_SLOTS)
    finalize(nq - 1)


def _diff_attn(z, lam_params, subln_g, *, batch, seq, dq, dv, lam_init, qb=512):
    kb = qb
    nq = seq // qb
    assert (nq * (nq + 1) // 2) % 2 == 0, "the pair loop needs an even number of (q block, k block) pairs"
    h = N_HEADS
    k_col0 = h * 2 * dq // LANES
    v_col0 = 2 * k_col0
    g_col0 = v_col0 + h * dv // LANES
    kern = functools.partial(_attn_kernel, qb=qb, kb=kb, dq=dq, row_chunk=32, lam_init=lam_init)
    return pl.pallas_call(
        kern,
        grid=(batch, h),
        in_specs=[
            pl.BlockSpec(lam_params.shape, lambda b, hh: (0, 0)),
            pl.BlockSpec((1, dv), lambda b, hh: (0, 0)),
            pl.BlockSpec((seq, 2 * dq), lambda b, hh: (b, hh)),
            pl.BlockSpec((seq, 2 * dq), lambda b, hh: (b, k_col0 + hh)),
            pl.BlockSpec((seq, dv), lambda b, hh: (b, v_col0 + hh)),
            pl.BlockSpec((seq, dv), lambda b, hh: (b, g_col0 + hh)),
        ],
        out_specs=pl.BlockSpec((seq, dv), lambda b, hh: (b, hh)),
        out_shape=jax.ShapeDtypeStruct((batch * seq, h * dv), BF16),
        scratch_shapes=[pltpu.VMEM((dv + ONES_ROWS, seq), BF16),
                        pltpu.VMEM((2 * qb, 2 * dq), BF16),
                        pltpu.VMEM((N_SLOTS, kb, 2 * qb), F32),
                        pltpu.VMEM((N_SLOTS, kb, 2 * qb), BF16),
                        pltpu.VMEM((N_SLOTS, 1, 2 * qb), F32),
                        pltpu.VMEM((1, 2 * qb), F32),
                        pltpu.VMEM((dv + ONES_ROWS, 2 * qb), F32)],
        compiler_params=pltpu.CompilerParams(
            dimension_semantics=("arbitrary", "arbitrary"),
            vmem_limit_bytes=VMEM_LIMIT),
        name="diff_attn",
    )(lam_params, subln_g, z, z, z, z)


def _conv_kernel(ua_ref, ub_ref, ha_ref, hb_ref, gc_ref, dww_ref, dwb_ref, lng_ref, lnb_ref,
                 wpw_ref, bpw_ref, o_ref, sh_ref, wb_ref, conv_ref, *, ts, row_chunk, lane_chunk):
    i = pl.program_id(1)
    c = ua_ref.shape[1]

    @pl.when((pl.program_id(0) == 0) & (i == 0))
    def _():
        for j in range(CONV_WIDTH):
            wb_ref[j] = jnp.broadcast_to(dww_ref[j:j + 1, :], (SUBLANES, c))

    def glu(a_ref, b_ref):
        return a_ref[...].astype(F32) * jax.nn.sigmoid(b_ref[...].astype(F32))

    hist = jnp.where(i > 0, glu(ha_ref, hb_ref), 0.0)
    ypad = jnp.concatenate([hist, glu(ua_ref, ub_ref)], axis=0)
    for r in range(SUBLANES):
        n_r = HALO + ts - (SUBLANES if r else 0)
        sh_ref[r, 0:n_r, 0:c] = ypad[r:r + n_r, :]

    base = HALO - (CONV_WIDTH - 1)
    n_rc = ts // row_chunk
    for lc in range(c // lane_chunk):
        l0 = lc * lane_chunk

        def body(rc, carry, l0=l0):
            r0 = pl.multiple_of(rc * row_chunk, row_chunk)
            parts = []
            for r in range(SUBLANES):
                taps = [j for j in range(CONV_WIDTH) if (base + j) % SUBLANES == r]
                a_lo, a_hi = (base + taps[0]) // SUBLANES, (base + taps[-1]) // SUBLANES
                n_win = row_chunk + (a_hi - a_lo) * SUBLANES
                win = sh_ref[r, pl.ds(r0 + a_lo * SUBLANES, n_win), l0:l0 + lane_chunk]
                win = win.reshape(n_win // SUBLANES, SUBLANES, lane_chunk)
                part = None
                for j in taps:
                    t0 = (base + j) // SUBLANES - a_lo
                    term = win[t0:t0 + row_chunk // SUBLANES] * wb_ref[j, :, l0:l0 + lane_chunk]
                    part = term if part is None else part + term
                parts.append(part)
                if len(parts) == 3:
                    parts = [parts[0] + (parts[1] + parts[2])]
            acc = parts[0] if len(parts) == 1 else parts[0] + parts[1]
            conv_ref[pl.ds(r0, row_chunk), l0:l0 + lane_chunk] = acc.reshape(row_chunk, lane_chunk)
            return carry

        lax.fori_loop(0, n_rc, body, 0)

    y = conv_ref[...] + dwb_ref[...]
    mu = jnp.mean(y, axis=-1, keepdims=True)
    var = jnp.mean(jnp.square(y - mu), axis=-1, keepdims=True)
    y = (y - mu) * lax.rsqrt(var + EPS) * lng_ref[...] + lnb_ref[...]
    y = _silu(y)
    y = jnp.dot(y.astype(BF16), wpw_ref[...], preferred_element_type=F32) + bpw_ref[...]
    o_ref[...] = (y * _silu(gc_ref[...].astype(F32))).astype(o_ref.dtype)


def _conv_mod(z, dw_w, dw_b, ln_g, ln_b, w_pw_bf, b_pw, *, batch, seq, c, ua_col0, ts=512):
    nt = seq // ts
    ua_blk, ub_blk, gc_blk = ua_col0 // c, ua_col0 // c + 1, ua_col0 // c + 2
    halo_per_ts = ts // HALO
    kern = functools.partial(_conv_kernel, ts=ts, row_chunk=64, lane_chunk=128)

    def halo_map(blk):
        return lambda b, i: (jnp.maximum((b * nt + i) * halo_per_ts - 1, 0), blk)

    row = lambda b, i: (0, 0)
    return pl.pallas_call(
        kern,
        grid=(batch, nt),
        in_specs=[
            pl.BlockSpec((ts, c), lambda b, i: (b * nt + i, ua_blk)),
            pl.BlockSpec((ts, c), lambda b, i: (b * nt + i, ub_blk)),
            pl.BlockSpec((HALO, c), halo_map(ua_blk)),
            pl.BlockSpec((HALO, c), halo_map(ub_blk)),
            pl.BlockSpec((ts, c), lambda b, i: (b * nt + i, gc_blk)),
            pl.BlockSpec((CONV_WIDTH, c), row),
            pl.BlockSpec((1, c), row),
            pl.BlockSpec((1, c), row),
            pl.BlockSpec((1, c), row),
            pl.BlockSpec((c, c), row),
            pl.BlockSpec((1, c), row),
        ],
        out_specs=pl.BlockSpec((ts, c), lambda b, i: (b * nt + i, 0)),
        out_shape=jax.ShapeDtypeStruct((batch * seq, c), BF16),
        scratch_shapes=[pltpu.VMEM((SUBLANES, HALO + ts, c + LANES), F32),
                        pltpu.VMEM((CONV_WIDTH, SUBLANES, c), F32),
                        pltpu.VMEM((ts, c), F32)],
        compiler_params=pltpu.CompilerParams(
            dimension_semantics=("arbitrary", "arbitrary"), vmem_limit_bytes=VMEM_LIMIT),
        name="conv_mod",
    )(z, z, z, z, z, dw_w, dw_b, ln_g, ln_b, w_pw_bf, b_pw)


def _outproj_kernel(ya_ref, yc_ref, wa_ref, wc_ref, x_ref, gate_ref, fg_ref, o_ref, *, final_norm):
    mixed = jnp.dot(ya_ref[...], wa_ref[...].astype(BF16), preferred_element_type=F32)
    mixed = mixed + jnp.dot(yc_ref[...], wc_ref[...].astype(BF16), preferred_element_type=F32)
    xn = x_ref[...] + gate_ref[0] * mixed
    if final_norm:
        ms = jnp.mean(xn * xn, axis=-1, keepdims=True)
        xn = xn * lax.rsqrt(ms + EPS) * fg_ref[...]
    o_ref[...] = xn


def _out_proj(y_attn, y_conv, w_out, x2, mod3, final_g, *, seq, final_norm, tm=512):
    m, d = x2.shape
    da, dc = y_attn.shape[1], y_conv.shape[1]
    blocks_per_seq = seq // tm
    return pl.pallas_call(
        functools.partial(_outproj_kernel, final_norm=final_norm),
        grid=(m // tm,),
        in_specs=[
            pl.BlockSpec((tm, da), lambda i: (i, 0)),
            pl.BlockSpec((tm, dc), lambda i: (i, 0)),
            pl.BlockSpec((da, d), lambda i: (0, 0)),
            pl.BlockSpec((dc, d), lambda i: (da // dc, 0)),
            pl.BlockSpec((tm, d), lambda i: (i, 0)),
            pl.BlockSpec((1, 1, d), lambda i: ((i // blocks_per_seq) * 3 + 2, 0, 0)),
            pl.BlockSpec((1, d), lambda i: (0, 0)),
        ],
        out_specs=pl.BlockSpec((tm, d), lambda i: (i, 0)),
        out_shape=jax.ShapeDtypeStruct((m, d), F32),
        compiler_params=pltpu.CompilerParams(
            dimension_semantics=("arbitrary",), vmem_limit_bytes=VMEM_LIMIT),
        name="out_proj",
    )(y_attn, y_conv, w_out, w_out, x2, mod3, final_g)


def kernel(x, c, positions, norm_g, w_ada, b_ada, w_in, lambda_q1, lambda_k1, lambda_q2, lambda_k2,
           subln_g, conv_dw_w, conv_dw_b, conv_ln_g, conv_ln_b, w_pw, b_pw, w_out, final_g):
    batch, seq, d = x.shape
    depth = w_in.shape[0]
    d_attn = d // 2
    d_conv = d - d_attn
    dv = d_attn // N_HEADS
    dq = dv // 2
    rot = dq // 4

    tabs = _rope_tables(positions, dq=dq, rot=rot)
    c_pad = jnp.zeros((SUBLANES, d), F32).at[:batch].set(c)

    x2 = x.reshape(batch * seq, d)
    for l in range(depth):
        lam_init = 0.8 - 0.6 * math.exp(-0.3 * l)
        mod = _ada_mod(c_pad, w_ada[l], b_ada[l].reshape(1, -1))
        mod3 = mod[:batch].reshape(batch * 3, 1, d)
        z = _in_proj(x2, norm_g[l].reshape(1, d), mod3, tabs, w_in[l],
                     seq=seq, dq=dq, rot=rot, d_qk=2 * d_attn)
        lam_params = jnp.stack([lambda_q1[l], lambda_k1[l], lambda_q2[l], lambda_k2[l]]).astype(F32)
        y_attn = _diff_attn(z, lam_params, subln_g[l].reshape(1, dv),
                            batch=batch, seq=seq, dq=dq, dv=dv, lam_init=lam_init)
        y_conv = _conv_mod(z, conv_dw_w[l], conv_dw_b[l].reshape(1, -1), conv_ln_g[l].reshape(1, -1),
                           conv_ln_b[l].reshape(1, -1), w_pw[l].astype(BF16), b_pw[l].reshape(1, -1),
                           batch=batch, seq=seq, c=d_conv, ua_col0=4 * d_attn)
        x2 = _out_proj(y_attn, y_conv, w_out[l], x2, mod3, final_g.reshape(1, d),
                       seq=seq, final_norm=(l == depth - 1))
    return x2.reshape(batch, seq, d)
```

```python
import functools
import math

import jax
import jax.numpy as jnp
from jax import lax
from jax.experimental import pallas as pl
from jax.experimental.pallas import tpu as pltpu

F32 = jnp.float32
BF16 = jnp.bfloat16

N_HEADS = 8
CONV_WIDTH = 31
ROPE_THETA = 500000.0
EPS = 1e-6
LANES = 128
SUBLANES = 8
HALO = 32
NEG_BIG = -1e30
ONES_ROWS = 16
LOG2E = math.log2(math.e)
N_SLOTS = 3
VMEM_LIMIT = 56 * 1024 * 1024


def _aligned(start, multiple):
    return start if isinstance(start, int) else pl.multiple_of(start, multiple)


def _silu(t):
    return t * jax.nn.sigmoid(t)


def _ada_kernel(c_ref, w_ref, b_ref, o_ref):
    ca = _silu(c_ref[...])
    o_ref[...] = jnp.dot(ca.astype(BF16), w_ref[...].astype(BF16),
                         preferred_element_type=F32) + b_ref[...]


def _ada_mod(c_pad, w_ada, b_ada, tn=1024):
    rows, d = c_pad.shape
    n = w_ada.shape[1]
    return pl.pallas_call(
        _ada_kernel,
        grid=(n // tn,),
        in_specs=[pl.BlockSpec((rows, d), lambda j: (0, 0)),
                  pl.BlockSpec((d, tn), lambda j: (0, j)),
                  pl.BlockSpec((1, tn), lambda j: (0, j))],
        out_specs=pl.BlockSpec((rows, tn), lambda j: (0, j)),
        out_shape=jax.ShapeDtypeStruct((rows, n), F32),
        compiler_params=pltpu.CompilerParams(
            dimension_semantics=("arbitrary",), vmem_limit_bytes=VMEM_LIMIT),
        name="ada_mod",
    )(c_pad, w_ada, b_ada)


def _rope_tab_kernel(pos_ref, freq_ref, cos_ref, sin_ref):
    ang = pos_ref[...].astype(F32) * freq_ref[...]
    cos_ref[...] = jnp.cos(ang)
    sin_ref[...] = jnp.sin(ang)


def _rope_tables(positions, *, dq, rot):
    n = positions.size
    nf = rot // 2
    inv_freq = ROPE_THETA ** (-jnp.arange(0, rot, 2, dtype=F32) / rot)
    freq_lane = jnp.tile(inv_freq, LANES // nf).reshape(1, LANES)
    pos_rep = jnp.repeat(positions.reshape(-1), nf).reshape(n * nf // LANES, LANES)
    cos, sin = pl.pallas_call(
        _rope_tab_kernel,
        out_shape=[jax.ShapeDtypeStruct(pos_rep.shape, F32)] * 2,
        name="rope_tab",
    )(pos_rep, freq_lane)
    cos, sin = cos.reshape(n, nf), sin.reshape(n, nf)
    one, zero = jnp.ones((n, dq - rot), F32), jnp.zeros((n, dq - rot), F32)
    tab_c = jnp.tile(jnp.concatenate([cos, cos, one], axis=1), (1, LANES // dq))
    tab_s = jnp.tile(jnp.concatenate([sin, sin, zero], axis=1), (1, LANES // dq))
    return tab_c, tab_s


def _inproj_kernel(xt_ref, xb_ref, g_ref, scale_ref, shift_ref, tc_ref, ts_ref, w_ref, o_ref,
                   h_ref, *, dq, rot, q_scale, n_rope_blocks, chunk, norm_rows_per_iter):
    j = pl.program_id(1)

    @pl.when(j == 0)
    def _():
        gs = g_ref[...] * (1.0 + scale_ref[0])
        sh = shift_ref[0]
        half = xt_ref.shape[0]
        for part, x_ref in enumerate((xt_ref, xb_ref)):

            def norm_rows(r, carry, part=part, x_ref=x_ref):
                r0 = pl.multiple_of(r * norm_rows_per_iter, norm_rows_per_iter)
                x = x_ref[pl.ds(r0, norm_rows_per_iter), :]
                ms = jnp.mean(x * x, axis=-1, keepdims=True)
                h_ref[pl.ds(part * half + r0, norm_rows_per_iter), :] = (
                    x * lax.rsqrt(ms + EPS) * gs + sh).astype(BF16)
                return carry

            lax.fori_loop(0, half // norm_rows_per_iter, norm_rows, 0, unroll=8)

    @pl.when(j < n_rope_blocks)
    def _():
        sc = jnp.where(j < n_rope_blocks // 2, q_scale, 1.0).astype(F32)
        cs, sn = tc_ref[...] * sc, ts_ref[...] * sc
        d = lax.broadcasted_iota(jnp.int32, sn.shape, 1) % dq
        su = jnp.where(d < rot // 2, sn, 0.0)
        sd = jnp.where(d >= rot // 2, sn, 0.0)
        for c in range(w_ref.shape[1] // chunk):
            acc = jnp.dot(h_ref[...], w_ref[:, c * chunk:(c + 1) * chunk].astype(BF16),
                          preferred_element_type=F32)
            for t in range(chunk // LANES):
                a = acc[:, t * LANES:(t + 1) * LANES]
                up = pltpu.roll(a, LANES - rot // 2, 1)
                dn = pltpu.roll(a, rot // 2, 1)
                col = c * chunk + t * LANES
                o_ref[:, col:col + LANES] = (a * cs - up * su + dn * sd).astype(o_ref.dtype)

    @pl.when(j >= n_rope_blocks)
    def _():
        o_ref[...] = jnp.dot(h_ref[...], w_ref[...].astype(BF16),
                             preferred_element_type=F32).astype(o_ref.dtype)


def _in_proj(x2, norm_g, mod3, tabs, w_in, *, seq, dq, rot, d_qk, tm=1024, tn=1024):
    m, d = x2.shape
    n = w_in.shape[1]
    blocks_per_seq = seq // tm
    kern = functools.partial(_inproj_kernel, dq=dq, rot=rot, q_scale=LOG2E / math.sqrt(dq),
                             n_rope_blocks=d_qk // tn, chunk=256, norm_rows_per_iter=16)
    tab_spec = pl.BlockSpec((tm, LANES), lambda i, j: (i, 0))
    n_rows, n_cols = m // tm, n // tn

    def x_half(part):
        switch = n_cols - 2 + part
        return pl.BlockSpec((tm // 2, d), lambda i, j: (
            2 * jnp.where(j >= switch, jnp.minimum(i + 1, n_rows - 1), i) + part, 0))

    return pl.pallas_call(
        kern,
        grid=(n_rows, n_cols),
        in_specs=[
            x_half(0), x_half(1),
            pl.BlockSpec((1, d), lambda i, j: (0, 0)),
            pl.BlockSpec((1, 1, d), lambda i, j: ((i // blocks_per_seq) * 3 + 1, 0, 0)),
            pl.BlockSpec((1, 1, d), lambda i, j: ((i // blocks_per_seq) * 3 + 0, 0, 0)),
            tab_spec, tab_spec,
            pl.BlockSpec((d, tn), lambda i, j: (0, j)),
        ],
        out_specs=pl.BlockSpec((tm, tn), lambda i, j: (i, j)),
        out_shape=jax.ShapeDtypeStruct((m, n), BF16),
        scratch_shapes=[pltpu.VMEM((tm, d), BF16)],
        compiler_params=pltpu.CompilerParams(
            dimension_semantics=("arbitrary", "arbitrary"), vmem_limit_bytes=VMEM_LIMIT),
        name="in_proj",
    )(x2, x2, norm_g, mod3, mod3, *tabs, w_in)


def _attn_kernel(lam_ref, sg_ref, q_ref, k_ref, v_ref, g_ref, o_ref,
                 vt_ref, qq_ref, s_ref, p_ref, al_ref, m_ref, acc_ref, *, qb, kb, dq, row_chunk, lam_init):
    seq, dv = v_ref.shape
    nq = seq // qb
    n_items = nq * (nq + 1) // 2

    vt_ref[0:dv, :] = v_ref[...].astype(F32).T.astype(BF16)
    vt_ref[dv:, :] = jnp.ones((vt_ref.shape[0] - dv, seq), BF16)

    lam = (jnp.exp(jnp.sum(lam_ref[0:1, :] * lam_ref[1:2, :], axis=-1, keepdims=True))
           - jnp.exp(jnp.sum(lam_ref[2:3, :] * lam_ref[3:4, :], axis=-1, keepdims=True))
           + lam_init)

    def build_qq(qi):
        q = q_ref[pl.ds(_aligned(qi * qb, qb), qb), :]
        lane = lax.broadcasted_iota(jnp.int32, q.shape, 1)
        zero = jnp.zeros_like(q)
        qq_ref[0:qb, :] = jnp.where(lane < dq, q, zero)
        qq_ref[qb:, :] = jnp.where(lane >= dq, q, zero)

    def scores(j, slot):
        k0 = _aligned(j * kb, kb)
        s = lax.dot_general(k_ref[pl.ds(k0, kb), :], qq_ref[...], (((1,), (1,)), ((), ())),
                            preferred_element_type=F32)
        s_ref[slot] = s

    def softmax(slot, masked):
        for c in range(2):
            cols = slice(c * qb, (c + 1) * qb)

            def chunk(r):
                s = s_ref[slot, r:r + row_chunk, cols]
                if masked:
                    kpos = r + lax.broadcasted_iota(jnp.int32, s.shape, 0)
                    qpos = lax.broadcasted_iota(jnp.int32, s.shape, 1)
                    s = jnp.where(kpos <= qpos, s, NEG_BIG)
                return s

            part = None
            for r in range(0, kb, row_chunk):
                t = jnp.max(chunk(r).reshape(row_chunk // SUBLANES, SUBLANES, qb), axis=0)
                part = t if part is None else jnp.maximum(part, t)
            m_old = m_ref[:, cols]
            m_new = jnp.maximum(m_old, jnp.max(part, axis=0, keepdims=True))
            al_ref[slot, :, cols] = jnp.exp2(m_old - m_new)
            m_ref[:, cols] = m_new
            for r in range(0, kb, row_chunk):
                p_ref[slot, r:r + row_chunk, cols] = jnp.exp2(chunk(r) - m_new).astype(BF16)

    def pv_update(j, slot):
        k0 = _aligned(j * kb, kb)
        pv = jnp.dot(vt_ref[:, pl.ds(k0, kb)], p_ref[slot], preferred_element_type=F32)
        acc_ref[...] = al_ref[slot] * acc_ref[...] + pv

    def finalize(qi):
        rows = pl.ds(_aligned(qi * qb, qb), qb)
        o_all = acc_ref[0:dv, :] / acc_ref[dv:dv + 1, :]
        o = o_all[:, :qb] - lam * o_all[:, qb:]
        ms = jnp.mean(o * o, axis=0, keepdims=True)
        on = (o * lax.rsqrt(ms + EPS)).T
        on = on * sg_ref[...] * (1.0 - lam_init)
        o_ref[rows, :] = (on * _silu(g_ref[rows, :].astype(F32))).astype(o_ref.dtype)

    pairs = [(qi, j) for qi in range(nq) for j in range(qi + 1)]
    build_qq(0)
    m_ref[...] = jnp.full(m_ref.shape, NEG_BIG, F32)
    acc_ref[...] = jnp.zeros(acc_ref.shape, F32)
    scores(0, 0)
    for n, (qi, j) in enumerate(pairs):
        slot, prev, nxt = n % N_SLOTS, (n - 1) % N_SLOTS, (n + 1) % N_SLOTS
        if j == qi and qi + 1 < nq:
            build_qq(qi + 1)
        softmax(slot, masked=(j == qi))
        if n > 0:
            pv_update(pairs[n - 1][1], prev)
        if n + 1 < len(pairs):
            scores(pairs[n + 1][1], nxt)
        if j == 0 and qi > 0:
            finalize(qi - 1)
            acc_ref[...] = jnp.zeros(acc_ref.shape, F32)
        if j == qi:
            m_ref[...] = jnp.full(m_ref.shape, NEG_BIG, F32)
    pv_update(pairs[-1][1], (len(pairs) - 1) % N_SLOTS)
    finalize(nq - 1)


def _diff_attn(z, lam_params, subln_g, *, batch, seq, dq, dv, lam_init, qb=512):
    kb = qb
    nq = seq // qb
    assert (nq * (nq + 1) // 2) % 2 == 0, "the pair loop needs an even number of (q block, k block) pairs"
    h = N_HEADS
    k_col0 = h * 2 * dq // LANES
    v_col0 = 2 * k_col0
    g_col0 = v_col0 + h * dv // LANES
    kern = functools.partial(_attn_kernel, qb=qb, kb=kb, dq=dq, row_chunk=32, lam_init=lam_init)
    return pl.pallas_call(
        kern,
        grid=(batch, h),
        in_specs=[
            pl.BlockSpec(lam_params.shape, lambda b, hh: (0, 0)),
            pl.BlockSpec((1, dv), lambda b, hh: (0, 0)),
            pl.BlockSpec((seq, 2 * dq), lambda b, hh: (b, hh)),
            pl.BlockSpec((seq, 2 * dq), lambda b, hh: (b, k_col0 + hh)),
            pl.BlockSpec((seq, dv), lambda b, hh: (b, v_col0 + hh)),
            pl.BlockSpec((seq, dv), lambda b, hh: (b, g_col0 + hh)),
        ],
        out_specs=pl.BlockSpec((seq, dv), lambda b, hh: (b, hh)),
        out_shape=jax.ShapeDtypeStruct((batch * seq, h * dv), BF16),
        scratch_shapes=[pltpu.VMEM((dv + ONES_ROWS, seq), BF16),
                        pltpu.VMEM((2 * qb, 2 * dq), BF16),
                        pltpu.VMEM((N_SLOTS, kb, 2 * qb), F32),
                        pltpu.VMEM((N_SLOTS, kb, 2 * qb), BF16),
                        pltpu.VMEM((N_SLOTS, 1, 2 * qb), F32),
                        pltpu.VMEM((1, 2 * qb), F32),
                        pltpu.VMEM((dv + ONES_ROWS, 2 * qb), F32)],
        compiler_params=pltpu.CompilerParams(
            dimension_semantics=("arbitrary", "arbitrary"),
            vmem_limit_bytes=VMEM_LIMIT),
        name="diff_attn",
    )(lam_params, subln_g, z, z, z, z)


def _conv_kernel(ua_ref, ub_ref, ha_ref, hb_ref, gc_ref, dww_ref, dwb_ref, lng_ref, lnb_ref,
                 wpw_ref, bpw_ref, o_ref, sh_ref, wb_ref, conv_ref, *, ts, row_chunk, lane_chunk):
    i = pl.program_id(1)
    c = ua_ref.shape[1]

    @pl.when((pl.program_id(0) == 0) & (i == 0))
    def _():
        for j in range(CONV_WIDTH):
            wb_ref[j] = jnp.broadcast_to(dww_ref[j:j + 1, :], (SUBLANES, c))

    def glu(a_ref, b_ref):
        return a_ref[...].astype(F32) * jax.nn.sigmoid(b_ref[...].astype(F32))

    hist = jnp.where(i > 0, glu(ha_ref, hb_ref), 0.0)
    ypad = jnp.concatenate([hist, glu(ua_ref, ub_ref)], axis=0)
    for r in range(SUBLANES):
        n_r = HALO + ts - (SUBLANES if r else 0)
        sh_ref[r, 0:n_r, 0:c] = ypad[r:r + n_r, :]

    base = HALO - (CONV_WIDTH - 1)
    n_rc = ts // row_chunk
    for lc in range(c // lane_chunk):
        l0 = lc * lane_chunk

        def body(rc, carry, l0=l0):
            r0 = pl.multiple_of(rc * row_chunk, row_chunk)
            parts = []
            for r in range(SUBLANES):
                taps = [j for j in range(CONV_WIDTH) if (base + j) % SUBLANES == r]
                a_lo, a_hi = (base + taps[0]) // SUBLANES, (base + taps[-1]) // SUBLANES
                n_win = row_chunk + (a_hi - a_lo) * SUBLANES
                win = sh_ref[r, pl.ds(r0 + a_lo * SUBLANES, n_win), l0:l0 + lane_chunk]
                win = win.reshape(n_win // SUBLANES, SUBLANES, lane_chunk)
                part = None
                for j in taps:
                    t0 = (base + j) // SUBLANES - a_lo
                    term = win[t0:t0 + row_chunk // SUBLANES] * wb_ref[j, :, l0:l0 + lane_chunk]
                    part = term if part is None else part + term
                parts.append(part)
                if len(parts) == 3:
                    parts = [parts[0] + (parts[1] + parts[2])]
            acc = parts[0] if len(parts) == 1 else parts[0] + parts[1]
            conv_ref[pl.ds(r0, row_chunk), l0:l0 + lane_chunk] = acc.reshape(row_chunk, lane_chunk)
            return carry

        lax.fori_loop(0, n_rc, body, 0)

    y = conv_ref[...] + dwb_ref[...]
    mu = jnp.mean(y, axis=-1, keepdims=True)
    var = jnp.mean(jnp.square(y - mu), axis=-1, keepdims=True)
    y = (y - mu) * lax.rsqrt(var + EPS) * lng_ref[...] + lnb_ref[...]
    y = _silu(y)
    y = jnp.dot(y.astype(BF16), wpw_ref[...], preferred_element_type=F32) + bpw_ref[...]
    o_ref[...] = (y * _silu(gc_ref[...].astype(F32))).astype(o_ref.dtype)


def _conv_mod(z, dw_w, dw_b, ln_g, ln_b, w_pw_bf, b_pw, *, batch, seq, c, ua_col0, ts=512):
    nt = seq // ts
    ua_blk, ub_blk, gc_blk = ua_col0 // c, ua_col0 // c + 1, ua_col0 // c + 2
    halo_per_ts = ts // HALO
    kern = functools.partial(_conv_kernel, ts=ts, row_chunk=64, lane_chunk=128)

    def halo_map(blk):
        return lambda b, i: (jnp.maximum((b * nt + i) * halo_per_ts - 1, 0), blk)

    row = lambda b, i: (0, 0)
    return pl.pallas_call(
        kern,
        grid=(batch, nt),
        in_specs=[
            pl.BlockSpec((ts, c), lambda b, i: (b * nt + i, ua_blk)),
            pl.BlockSpec((ts, c), lambda b, i: (b * nt + i, ub_blk)),
            pl.BlockSpec((HALO, c), halo_map(ua_blk)),
            pl.BlockSpec((HALO, c), halo_map(ub_blk)),
            pl.BlockSpec((ts, c), lambda b, i: (b * nt + i, gc_blk)),
            pl.BlockSpec((CONV_WIDTH, c), row),
            pl.BlockSpec((1, c), row),
            pl.BlockSpec((1, c), row),
            pl.BlockSpec((1, c), row),
            pl.BlockSpec((c, c), row),
            pl.BlockSpec((1, c), row),
        ],
        out_specs=pl.BlockSpec((ts, c), lambda b, i: (b * nt + i, 0)),
        out_shape=jax.ShapeDtypeStruct((batch * seq, c), BF16),
        scratch_shapes=[pltpu.VMEM((SUBLANES, HALO + ts, c + LANES), F32),
                        pltpu.VMEM((CONV_WIDTH, SUBLANES, c), F32),
                        pltpu.VMEM((ts, c), F32)],
        compiler_params=pltpu.CompilerParams(
            dimension_semantics=("arbitrary", "arbitrary"), vmem_limit_bytes=VMEM_LIMIT),
        name="conv_mod",
    )(z, z, z, z, z, dw_w, dw_b, ln_g, ln_b, w_pw_bf, b_pw)


def _outproj_kernel(ya_ref, yc_ref, wa_ref, wc_ref, x_ref, gate_ref, fg_ref, o_ref, *, final_norm):
    mixed = jnp.dot(ya_ref[...], wa_ref[...].astype(BF16), preferred_element_type=F32)
    mixed = mixed + jnp.dot(yc_ref[...], wc_ref[...].astype(BF16), preferred_element_type=F32)
    xn = x_ref[...] + gate_ref[0] * mixed
    if final_norm:
        ms = jnp.mean(xn * xn, axis=-1, keepdims=True)
        xn = xn * lax.rsqrt(ms + EPS) * fg_ref[...]
    o_ref[...] = xn


def _out_proj(y_attn, y_conv, w_out, x2, mod3, final_g, *, seq, final_norm, tm=512):
    m, d = x2.shape
    da, dc = y_attn.shape[1], y_conv.shape[1]
    blocks_per_seq = seq // tm
    return pl.pallas_call(
        functools.partial(_outproj_kernel, final_norm=final_norm),
        grid=(m // tm,),
        in_specs=[
            pl.BlockSpec((tm, da), lambda i: (i, 0)),
            pl.BlockSpec((tm, dc), lambda i: (i, 0)),
            pl.BlockSpec((da, d), lambda i: (0, 0)),
            pl.BlockSpec((dc, d), lambda i: (da // dc, 0)),
            pl.BlockSpec((tm, d), lambda i: (i, 0)),
            pl.BlockSpec((1, 1, d), lambda i: ((i // blocks_per_seq) * 3 + 2, 0, 0)),
            pl.BlockSpec((1, d), lambda i: (0, 0)),
        ],
        out_specs=pl.BlockSpec((tm, d), lambda i: (i, 0)),
        out_shape=jax.ShapeDtypeStruct((m, d), F32),
        compiler_params=pltpu.CompilerParams(
            dimension_semantics=("arbitrary",), vmem_limit_bytes=VMEM_LIMIT),
        name="out_proj",
    )(y_attn, y_conv, w_out, w_out, x2, mod3, final_g)


def kernel(x, c, positions, norm_g, w_ada, b_ada, w_in, lambda_q1, lambda_k1, lambda_q2, lambda_k2,
           subln_g, conv_dw_w, conv_dw_b, conv_ln_g, conv_ln_b, w_pw, b_pw, w_out, final_g):
    batch, seq, d = x.shape
    depth = w_in.shape[0]
    d_attn = d // 2
    d_conv = d - d_attn
    dv = d_attn // N_HEADS
    dq = dv // 2
    rot = dq // 4

    tabs = _rope_tables(positions, dq=dq, rot=rot)
    c_pad = jnp.zeros((SUBLANES, d), F32).at[:batch].set(c)

    x2 = x.reshape(batch * seq, d)
    for l in range(depth):
        lam_init = 0.8 - 0.6 * math.exp(-0.3 * l)
        mod = _ada_mod(c_pad, w_ada[l], b_ada[l].reshape(1, -1))
        mod3 = mod[:batch].reshape(batch * 3, 1, d)
        z = _in_proj(x2, norm_g[l].reshape(1, d), mod3, tabs, w_in[l],
                     seq=seq, dq=dq, rot=rot, d_qk=2 * d_attn)
        lam_params = jnp.stack([lambda_q1[l], lambda_k1[l], lambda_q2[l], lambda_k2[l]]).astype(F32)
        y_attn = _diff_attn(z, lam_params, subln_g[l].reshape(1, dv),
                            batch=batch, seq=seq, dq=dq, dv=dv, lam_init=lam_init)
        y_conv = _conv_mod(z, conv_dw_w[l], conv_dw_b[l].reshape(1, -1), conv_ln_g[l].reshape(1, -1),
                           conv_ln_b[l].reshape(1, -1), w_pw[l].astype(BF16), b_pw[l].reshape(1, -1),
                           batch=batch, seq=seq, c=d_conv, ua_col0=4 * d_attn)
        x2 = _out_proj(y_attn, y_conv, w_out[l], x2, mod3, final_g.reshape(1, d),
                       seq=seq, final_norm=(l == depth - 1))
    return x2.reshape(batch, seq, d)
```

```python
import functools
import math

import jax
import jax.numpy as jnp
from jax import lax
from jax.experimental import pallas as pl
from jax.experimental.pallas import tpu as pltpu

F32 = jnp.float32
BF16 = jnp.bfloat16

N_HEADS = 8
CONV_WIDTH = 31
ROPE_THETA = 500000.0
EPS = 1e-6
LANES = 128
SUBLANES = 8
HALO = 32
NEG_BIG = -1e30
ONES_ROWS = 16
LOG2E = math.log2(math.e)
N_SLOTS = 3
VMEM_LIMIT = 56 * 1024 * 1024


def _aligned(start, multiple):
    return start if isinstance(start, int) else pl.multiple_of(start, multiple)


def _silu(t):
    return t * jax.nn.sigmoid(t)


def _ada_kernel(c_ref, wl_ref, wr_ref, bl_ref, br_ref, ol_ref, or_ref):
    ca = _silu(c_ref[...]).astype(BF16)
    ol_ref[...] = jnp.dot(ca, wl_ref[...].astype(BF16), preferred_element_type=F32) + bl_ref[...]
    or_ref[...] = jnp.dot(ca, wr_ref[...].astype(BF16), preferred_element_type=F32) + br_ref[...]


def _ada_mod(c_pad, w_ada, b_ada, tn=1024):
    rows, d = c_pad.shape
    n = w_ada.shape[1]
    steps = n // (2 * tn)
    left, right = (lambda j: (0, j)), (lambda j: (0, j + steps))
    out_l, out_r = pl.pallas_call(
        _ada_kernel,
        grid=(steps,),
        in_specs=[pl.BlockSpec((rows, d), lambda j: (0, 0)),
                  pl.BlockSpec((d, tn), left), pl.BlockSpec((d, tn), right),
                  pl.BlockSpec((1, tn), left), pl.BlockSpec((1, tn), right)],
        out_specs=[pl.BlockSpec((rows, tn), left), pl.BlockSpec((rows, tn), left)],
        out_shape=[jax.ShapeDtypeStruct((rows, n // 2), F32)] * 2,
        compiler_params=pltpu.CompilerParams(
            dimension_semantics=("arbitrary",), vmem_limit_bytes=VMEM_LIMIT),
        name="ada_mod",
    )(c_pad, w_ada, w_ada, b_ada, b_ada)
    return jnp.concatenate([out_l, out_r], axis=1)


def _rope_tab_kernel(pos_ref, freq_ref, cos_ref, sin_ref):
    ang = pos_ref[...].astype(F32) * freq_ref[...]
    cos_ref[...] = jnp.cos(ang)
    sin_ref[...] = jnp.sin(ang)


def _rope_tables(positions, *, dq, rot):
    n = positions.size
    nf = rot // 2
    inv_freq = ROPE_THETA ** (-jnp.arange(0, rot, 2, dtype=F32) / rot)
    freq_lane = jnp.tile(inv_freq, LANES // nf).reshape(1, LANES)
    pos_rep = jnp.repeat(positions.reshape(-1), nf).reshape(n * nf // LANES, LANES)
    cos, sin = pl.pallas_call(
        _rope_tab_kernel,
        out_shape=[jax.ShapeDtypeStruct(pos_rep.shape, F32)] * 2,
        name="rope_tab",
    )(pos_rep, freq_lane)
    rotated = (jnp.arange(LANES) % dq < rot)[None, :]
    lanes = lambda t: jnp.broadcast_to(t.reshape(n, 1, nf), (n, LANES // nf, nf)).reshape(n, LANES)
    return jnp.where(rotated, lanes(cos), 1.0), jnp.where(rotated, lanes(sin), 0.0)


def _inproj_kernel(xt_ref, xb_ref, g_ref, scale_ref, shift_ref, tc_ref, ts_ref, w_ref, o_ref,
                   h_ref, *, dq, rot, q_scale, n_rope_blocks, chunk, norm_rows_per_iter):
    j = pl.program_id(1)

    @pl.when(j == 0)
    def _():
        gs = g_ref[...] * (1.0 + scale_ref[0])
        sh = shift_ref[0]
        half = xt_ref.shape[0]
        for part, x_ref in enumerate((xt_ref, xb_ref)):

            def norm_rows(r, carry, part=part, x_ref=x_ref):
                r0 = pl.multiple_of(r * norm_rows_per_iter, norm_rows_per_iter)
                x = x_ref[pl.ds(r0, norm_rows_per_iter), :]
                ms = jnp.mean(x * x, axis=-1, keepdims=True)
                h_ref[pl.ds(part * half + r0, norm_rows_per_iter), :] = (
                    x * lax.rsqrt(ms + EPS) * gs + sh).astype(BF16)
                return carry

            lax.fori_loop(0, half // norm_rows_per_iter, norm_rows, 0, unroll=8)

    @pl.when(j < n_rope_blocks)
    def _():
        sc = jnp.where(j < n_rope_blocks // 2, q_scale, 1.0).astype(F32)
        cs, sn = tc_ref[...] * sc, ts_ref[...] * sc
        d = lax.broadcasted_iota(jnp.int32, sn.shape, 1) % dq
        su = jnp.where(d < rot // 2, sn, 0.0)
        sd = jnp.where(d >= rot // 2, sn, 0.0)
        for c in range(w_ref.shape[1] // chunk):
            acc = jnp.dot(h_ref[...], w_ref[:, c * chunk:(c + 1) * chunk].astype(BF16),
                          preferred_element_type=F32)
            for t in range(chunk // LANES):
                a = acc[:, t * LANES:(t + 1) * LANES]
                up = pltpu.roll(a, LANES - rot // 2, 1)
                dn = pltpu.roll(a, rot // 2, 1)
                col = c * chunk + t * LANES
                o_ref[:, col:col + LANES] = (a * cs - up * su + dn * sd).astype(o_ref.dtype)

    @pl.when(j >= n_rope_blocks)
    def _():
        o_ref[...] = jnp.dot(h_ref[...], w_ref[...].astype(BF16),
                             preferred_element_type=F32).astype(o_ref.dtype)


def _in_proj(x2, norm_g, mod3, tabs, w_in, *, seq, dq, rot, d_qk, tm=1024, tn=1024):
    m, d = x2.shape
    n = w_in.shape[1]
    blocks_per_seq = seq // tm
    kern = functools.partial(_inproj_kernel, dq=dq, rot=rot, q_scale=LOG2E / math.sqrt(dq),
                             n_rope_blocks=d_qk // tn, chunk=256, norm_rows_per_iter=16)
    tab_spec = pl.BlockSpec((tm, LANES), lambda i, j: (i, 0))
    n_rows, n_cols = m // tm, n // tn

    def x_half(part):
        switch = n_cols - 2 + part
        return pl.BlockSpec((tm // 2, d), lambda i, j: (
            2 * jnp.where(j >= switch, jnp.minimum(i + 1, n_rows - 1), i) + part, 0))

    return pl.pallas_call(
        kern,
        grid=(n_rows, n_cols),
        in_specs=[
            x_half(0), x_half(1),
            pl.BlockSpec((1, d), lambda i, j: (0, 0)),
            pl.BlockSpec((1, 1, d), lambda i, j: ((i // blocks_per_seq) * 3 + 1, 0, 0)),
            pl.BlockSpec((1, 1, d), lambda i, j: ((i // blocks_per_seq) * 3 + 0, 0, 0)),
            tab_spec, tab_spec,
            pl.BlockSpec((d, tn), lambda i, j: (0, j)),
        ],
        out_specs=pl.BlockSpec((tm, tn), lambda i, j: (i, j)),
        out_shape=jax.ShapeDtypeStruct((m, n), BF16),
        scratch_shapes=[pltpu.VMEM((tm, d), BF16)],
        compiler_params=pltpu.CompilerParams(
            dimension_semantics=("arbitrary", "arbitrary"), vmem_limit_bytes=VMEM_LIMIT),
        name="in_proj",
    )(x2, x2, norm_g, mod3, mod3, *tabs, w_in)


def _attn_kernel(lam_ref, sg_ref, q_ref, k_ref, v_ref, g_ref, o_ref,
                 vt_ref, qq_ref, s_ref, p_ref, al_ref, m_ref, acc_ref, *, qb, kb, dq, row_chunk, lam_init):
    seq, dv = v_ref.shape
    nq = seq // qb
    n_items = nq * (nq + 1) // 2

    vt_ref[0:dv, :] = v_ref[...].astype(F32).T.astype(BF16)
    vt_ref[dv:, :] = jnp.ones((vt_ref.shape[0] - dv, seq), BF16)

    lam = (jnp.exp(jnp.sum(lam_ref[0:1, :] * lam_ref[1:2, :], axis=-1, keepdims=True))
           - jnp.exp(jnp.sum(lam_ref[2:3, :] * lam_ref[3:4, :], axis=-1, keepdims=True))
           + lam_init)

    def build_qq(qi):
        q = q_ref[pl.ds(_aligned(qi * qb, qb), qb), :]
        lane = lax.broadcasted_iota(jnp.int32, q.shape, 1)
        zero = jnp.zeros_like(q)
        qq_ref[0:qb, :] = jnp.where(lane < dq, q, zero)
        qq_ref[qb:, :] = jnp.where(lane >= dq, q, zero)

    def scores(j, slot):
        k0 = _aligned(j * kb, kb)
        s = lax.dot_general(k_ref[pl.ds(k0, kb), :], qq_ref[...], (((1,), (1,)), ((), ())),
                            preferred_element_type=F32)
        s_ref[slot] = s

    def softmax(slot, masked):
        for c in range(2):
            cols = slice(c * qb, (c + 1) * qb)

            def chunk(r):
                s = s_ref[slot, r:r + row_chunk, cols]
                if masked:
                    kpos = r + lax.broadcasted_iota(jnp.int32, s.shape, 0)
                    qpos = lax.broadcasted_iota(jnp.int32, s.shape, 1)
                    s = jnp.where(kpos <= qpos, s, NEG_BIG)
                return s

            part = None
            for r in range(0, kb, row_chunk):
                t = jnp.max(chunk(r).reshape(row_chunk // SUBLANES, SUBLANES, qb), axis=0)
                part = t if part is None else jnp.maximum(part, t)
            m_old = m_ref[:, cols]
            m_new = jnp.maximum(m_old, jnp.max(part, axis=0, keepdims=True))
            al_ref[slot, :, cols] = jnp.exp2(m_old - m_new)
            m_ref[:, cols] = m_new
            for r in range(0, kb, row_chunk):
                p_ref[slot, r:r + row_chunk, cols] = jnp.exp2(chunk(r) - m_new).astype(BF16)

    def pv_update(j, slot):
        k0 = _aligned(j * kb, kb)
        pv = jnp.dot(vt_ref[:, pl.ds(k0, kb)], p_ref[slot], preferred_element_type=F32)
        acc_ref[...] = al_ref[slot] * acc_ref[...] + pv

    def finalize(qi):
        rows = pl.ds(_aligned(qi * qb, qb), qb)
        o_all = acc_ref[0:dv, :] / acc_ref[dv:dv + 1, :]
        o = o_all[:, :qb] - lam * o_all[:, qb:]
        ms = jnp.mean(o * o, axis=0, keepdims=True)
        on = (o * lax.rsqrt(ms + EPS)).T
        on = on * sg_ref[...] * (1.0 - lam_init)
        o_ref[rows, :] = (on * _silu(g_ref[rows, :].astype(F32))).astype(o_ref.dtype)

    pairs = [(qi, j) for qi in range(nq) for j in range(qi + 1)]
    build_qq(0)
    m_ref[...] = jnp.full(m_ref.shape, NEG_BIG, F32)
    acc_ref[...] = jnp.zeros(acc_ref.shape, F32)
    scores(0, 0)
    for n, (qi, j) in enumerate(pairs):
        slot, prev, nxt = n % N_SLOTS, (n - 1) % N_SLOTS, (n + 1) % N_SLOTS
        if j == qi and qi + 1 < nq:
            build_qq(qi + 1)
        softmax(slot, masked=(j == qi))
        if n > 0:
            pv_update(pairs[n - 1][1], prev)
        if n + 1 < len(pairs):
            scores(pairs[n + 1][1], nxt)
        if j == 0 and qi > 0:
            finalize(qi - 1)
            acc_ref[...] = jnp.zeros(acc_ref.shape, F32)
        if j == qi:
            m_ref[...] = jnp.full(m_ref.shape, NEG_BIG, F32)
    pv_update(pairs[-1][1], (len(pairs) - 1) % N_SLOTS)
    finalize(nq - 1)


def _diff_attn(z, lam_params, subln_g, *, batch, seq, dq, dv, lam_init, qb=512):
    kb = qb
    nq = seq // qb
    assert (nq * (nq + 1) // 2) % 2 == 0, "the pair loop needs an even number of (q block, k block) pairs"
    h = N_HEADS
    k_col0 = h * 2 * dq // LANES
    v_col0 = 2 * k_col0
    g_col0 = v_col0 + h * dv // LANES
    kern = functools.partial(_attn_kernel, qb=qb, kb=kb, dq=dq, row_chunk=32, lam_init=lam_init)
    return pl.pallas_call(
        kern,
        grid=(batch, h),
        in_specs=[
            pl.BlockSpec(lam_params.shape, lambda b, hh: (0, 0)),
            pl.BlockSpec((1, dv), lambda b, hh: (0, 0)),
            pl.BlockSpec((seq, 2 * dq), lambda b, hh: (b, hh)),
            pl.BlockSpec((seq, 2 * dq), lambda b, hh: (b, k_col0 + hh)),
            pl.BlockSpec((seq, dv), lambda b, hh: (b, v_col0 + hh)),
            pl.BlockSpec((seq, dv), lambda b, hh: (b, g_col0 + hh)),
        ],
        out_specs=pl.BlockSpec((seq, dv), lambda b, hh: (b, hh)),
        out_shape=jax.ShapeDtypeStruct((batch * seq, h * dv), BF16),
        scratch_shapes=[pltpu.VMEM((dv + ONES_ROWS, seq), BF16),
                        pltpu.VMEM((2 * qb, 2 * dq), BF16),
                        pltpu.VMEM((N_SLOTS, kb, 2 * qb), F32),
                        pltpu.VMEM((N_SLOTS, kb, 2 * qb), BF16),
                        pltpu.VMEM((N_SLOTS, 1, 2 * qb), F32),
                        pltpu.VMEM((1, 2 * qb), F32),
                        pltpu.VMEM((dv + ONES_ROWS, 2 * qb), F32)],
        compiler_params=pltpu.CompilerParams(
            dimension_semantics=("arbitrary", "arbitrary"),
            vmem_limit_bytes=VMEM_LIMIT),
        name="diff_attn",
    )(lam_params, subln_g, z, z, z, z)


def _conv_kernel(ua_ref, ub_ref, ha_ref, hb_ref, gc_ref, dww_ref, dwb_ref, lng_ref, lnb_ref,
                 wpw_ref, bpw_ref, o_ref, sh_ref, wb_ref, conv_ref, *, ts, row_chunk, lane_chunk):
    i = pl.program_id(1)
    c = ua_ref.shape[1]

    @pl.when((pl.program_id(0) == 0) & (i == 0))
    def _():
        for j in range(CONV_WIDTH):
            wb_ref[j] = jnp.broadcast_to(dww_ref[j:j + 1, :], (SUBLANES, c))

    def glu(a_ref, b_ref):
        return a_ref[...].astype(F32) * jax.nn.sigmoid(b_ref[...].astype(F32))

    hist = jnp.where(i > 0, glu(ha_ref, hb_ref), 0.0)
    ypad = jnp.concatenate([hist, glu(ua_ref, ub_ref)], axis=0)
    for r in range(SUBLANES):
        n_r = HALO + ts - (SUBLANES if r else 0)
        sh_ref[r, 0:n_r, 0:c] = ypad[r:r + n_r, :]

    base = HALO - (CONV_WIDTH - 1)
    n_rc = ts // row_chunk
    for lc in range(c // lane_chunk):
        l0 = lc * lane_chunk

        def body(rc, carry, l0=l0):
            r0 = pl.multiple_of(rc * row_chunk, row_chunk)
            parts = []
            for r in range(SUBLANES):
                taps = [j for j in range(CONV_WIDTH) if (base + j) % SUBLANES == r]
                a_lo, a_hi = (base + taps[0]) // SUBLANES, (base + taps[-1]) // SUBLANES
                n_win = row_chunk + (a_hi - a_lo) * SUBLANES
                win = sh_ref[r, pl.ds(r0 + a_lo * SUBLANES, n_win), l0:l0 + lane_chunk]
                win = win.reshape(n_win // SUBLANES, SUBLANES, lane_chunk)
                part = None
                for j in taps:
                    t0 = (base + j) // SUBLANES - a_lo
                    term = win[t0:t0 + row_chunk // SUBLANES] * wb_ref[j, :, l0:l0 + lane_chunk]
                    part = term if part is None else part + term
                parts.append(part)
                if len(parts) == 3:
                    parts = [parts[0] + (parts[1] + parts[2])]
            acc = parts[0] if len(parts) == 1 else parts[0] + parts[1]
            conv_ref[pl.ds(r0, row_chunk), l0:l0 + lane_chunk] = acc.reshape(row_chunk, lane_chunk)
            return carry

        lax.fori_loop(0, n_rc, body, 0)

    y = conv_ref[...] + dwb_ref[...]
    mu = jnp.mean(y, axis=-1, keepdims=True)
    var = jnp.mean(jnp.square(y - mu), axis=-1, keepdims=True)
    y = (y - mu) * lax.rsqrt(var + EPS) * lng_ref[...] + lnb_ref[...]
    y = _silu(y)
    y = jnp.dot(y.astype(BF16), wpw_ref[...], preferred_element_type=F32) + bpw_ref[...]
    o_ref[...] = (y * _silu(gc_ref[...].astype(F32))).astype(o_ref.dtype)


def _conv_mod(z, dw_w, dw_b, ln_g, ln_b, w_pw_bf, b_pw, *, batch, seq, c, ua_col0, ts=512):
    nt = seq // ts
    ua_blk, ub_blk, gc_blk = ua_col0 // c, ua_col0 // c + 1, ua_col0 // c + 2
    halo_per_ts = ts // HALO
    kern = functools.partial(_conv_kernel, ts=ts, row_chunk=64, lane_chunk=128)

    def halo_map(blk):
        return lambda b, i: (jnp.maximum((b * nt + i) * halo_per_ts - 1, 0), blk)

    row = lambda b, i: (0, 0)
    return pl.pallas_call(
        kern,
        grid=(batch, nt),
        in_specs=[
            pl.BlockSpec((ts, c), lambda b, i: (b * nt + i, ua_blk)),
            pl.BlockSpec((ts, c), lambda b, i: (b * nt + i, ub_blk)),
            pl.BlockSpec((HALO, c), halo_map(ua_blk)),
            pl.BlockSpec((HALO, c), halo_map(ub_blk)),
            pl.BlockSpec((ts, c), lambda b, i: (b * nt + i, gc_blk)),
            pl.BlockSpec((CONV_WIDTH, c), row),
            pl.BlockSpec((1, c), row),
            pl.BlockSpec((1, c), row),
            pl.BlockSpec((1, c), row),
            pl.BlockSpec((c, c), row),
            pl.BlockSpec((1, c), row),
        ],
        out_specs=pl.BlockSpec((ts, c), lambda b, i: (b * nt + i, 0)),
        out_shape=jax.ShapeDtypeStruct((batch * seq, c), BF16),
        scratch_shapes=[pltpu.VMEM((SUBLANES, HALO + ts, c + LANES), F32),
                        pltpu.VMEM((CONV_WIDTH, SUBLANES, c), F32),
                        pltpu.VMEM((ts, c), F32)],
        compiler_params=pltpu.CompilerParams(
            dimension_semantics=("arbitrary", "arbitrary"), vmem_limit_bytes=VMEM_LIMIT),
        name="conv_mod",
    )(z, z, z, z, z, dw_w, dw_b, ln_g, ln_b, w_pw_bf, b_pw)


def _outproj_kernel(ya_ref, yc_ref, wa_ref, wc_ref, x_ref, gate_ref, fg_ref, o_ref, *, final_norm):
    mixed = jnp.dot(ya_ref[...], wa_ref[...].astype(BF16), preferred_element_type=F32)
    mixed = mixed + jnp.dot(yc_ref[...], wc_ref[...].astype(BF16), preferred_element_type=F32)
    xn = x_ref[...] + gate_ref[0] * mixed
    if final_norm:
        ms = jnp.mean(xn * xn, axis=-1, keepdims=True)
        xn = xn * lax.rsqrt(ms + EPS) * fg_ref[...]
    o_ref[...] = xn


def _out_proj(y_attn, y_conv, w_out, x2, mod3, final_g, *, seq, final_norm, tm=512):
    m, d = x2.shape
    da, dc = y_attn.shape[1], y_conv.shape[1]
    blocks_per_seq = seq // tm
    return pl.pallas_call(
        functools.partial(_outproj_kernel, final_norm=final_norm),
        grid=(m // tm,),
        in_specs=[
            pl.BlockSpec((tm, da), lambda i: (i, 0)),
            pl.BlockSpec((tm, dc), lambda i: (i, 0)),
            pl.BlockSpec((da, d), lambda i: (0, 0)),
            pl.BlockSpec((dc, d), lambda i: (da // dc, 0)),
            pl.BlockSpec((tm, d), lambda i: (i, 0)),
            pl.BlockSpec((1, 1, d), lambda i: ((i // blocks_per_seq) * 3 + 2, 0, 0)),
            pl.BlockSpec((1, d), lambda i: (0, 0)),
        ],
        out_specs=pl.BlockSpec((tm, d), lambda i: (i, 0)),
        out_shape=jax.ShapeDtypeStruct((m, d), F32),
        compiler_params=pltpu.CompilerParams(
            dimension_semantics=("arbitrary",), vmem_limit_bytes=VMEM_LIMIT),
        name="out_proj",
    )(y_attn, y_conv, w_out, w_out, x2, mod3, final_g)


def kernel(x, c, positions, norm_g, w_ada, b_ada, w_in, lambda_q1, lambda_k1, lambda_q2, lambda_k2,
           subln_g, conv_dw_w, conv_dw_b, conv_ln_g, conv_ln_b, w_pw, b_pw, w_out, final_g):
    batch, seq, d = x.shape
    depth = w_in.shape[0]
    d_attn = d // 2
    d_conv = d - d_attn
    dv = d_attn // N_HEADS
    dq = dv // 2
    rot = dq // 4

    tabs = _rope_tables(positions, dq=dq, rot=rot)
    c_pad = jnp.zeros((SUBLANES, d), F32).at[:batch].set(c)

    x2 = x.reshape(batch * seq, d)
    for l in range(depth):
        lam_init = 0.8 - 0.6 * math.exp(-0.3 * l)
        mod = _ada_mod(c_pad, w_ada[l], b_ada[l].reshape(1, -1))
        mod3 = mod[:batch].reshape(batch * 3, 1, d)
        z = _in_proj(x2, norm_g[l].reshape(1, d), mod3, tabs, w_in[l],
                     seq=seq, dq=dq, rot=rot, d_qk=2 * d_attn)
        lam_params = jnp.stack([lambda_q1[l], lambda_k1[l], lambda_q2[l], lambda_k2[l]]).astype(F32)
        y_attn = _diff_attn(z, lam_params, subln_g[l].reshape(1, dv),
                            batch=batch, seq=seq, dq=dq, dv=dv, lam_init=lam_init)
        y_conv = _conv_mod(z, conv_dw_w[l], conv_dw_b[l].reshape(1, -1), conv_ln_g[l].reshape(1, -1),
                           conv_ln_b[l].reshape(1, -1), w_pw[l].astype(BF16), b_pw[l].reshape(1, -1),
                           batch=batch, seq=seq, c=d_conv, ua_col0=4 * d_attn)
        x2 = _out_proj(y_attn, y_conv, w_out[l], x2, mod3, final_g.reshape(1, d),
                       seq=seq, final_norm=(l == depth - 1))
    return x2.reshape(batch, seq, d)
```

```python
import functools
import math

import jax
import jax.numpy as jnp
from jax import lax
from jax.experimental import pallas as pl
from jax.experimental.pallas import tpu as pltpu

F32 = jnp.float32
BF16 = jnp.bfloat16

N_HEADS = 8
CONV_WIDTH = 31
ROPE_THETA = 500000.0
EPS = 1e-6
LANES = 128
SUBLANES = 8
HALO = 32
NEG_BIG = -1e30
ONES_ROWS = 16
LOG2E = math.log2(math.e)
N_SLOTS = 3
VMEM_LIMIT = 56 * 1024 * 1024


def _aligned(start, multiple):
    return start if isinstance(start, int) else pl.multiple_of(start, multiple)


def _silu(t):
    return t * jax.nn.sigmoid(t)


def _ada_kernel(c_ref, w_ref, b_ref, o_ref):
    ca = _silu(c_ref[...])
    o_ref[...] = jnp.dot(ca.astype(BF16), w_ref[...].astype(BF16),
                         preferred_element_type=F32) + b_ref[...]


def _ada_mod(c_pad, w_ada, b_ada, tn=1024):
    rows, d = c_pad.shape
    n = w_ada.shape[1]
    return pl.pallas_call(
        _ada_kernel,
        grid=(n // tn,),
        in_specs=[pl.BlockSpec((rows, d), lambda j: (0, 0)),
                  pl.BlockSpec((d, tn), lambda j: (0, j)),
                  pl.BlockSpec((1, tn), lambda j: (0, j))],
        out_specs=pl.BlockSpec((rows, tn), lambda j: (0, j)),
        out_shape=jax.ShapeDtypeStruct((rows, n), F32),
        compiler_params=pltpu.CompilerParams(
            dimension_semantics=("arbitrary",), vmem_limit_bytes=VMEM_LIMIT),
        name="ada_mod",
    )(c_pad, w_ada, b_ada)


def _rope_tab_kernel(pos_ref, freq_ref, cos_ref, sin_ref):
    ang = pos_ref[...].astype(F32) * freq_ref[...]
    cos_ref[...] = jnp.cos(ang)
    sin_ref[...] = jnp.sin(ang)


def _rope_tables(positions, *, dq, rot):
    n = positions.size
    nf = rot // 2
    inv_freq = ROPE_THETA ** (-jnp.arange(0, rot, 2, dtype=F32) / rot)
    freq_lane = jnp.tile(inv_freq, LANES // nf).reshape(1, LANES)
    pos_rep = jnp.repeat(positions.reshape(-1), nf).reshape(n * nf // LANES, LANES)
    cos, sin = pl.pallas_call(
        _rope_tab_kernel,
        out_shape=[jax.ShapeDtypeStruct(pos_rep.shape, F32)] * 2,
        name="rope_tab",
    )(pos_rep, freq_lane)
    cos, sin = cos.reshape(n, nf), sin.reshape(n, nf)
    one, zero = jnp.ones((n, dq - rot), F32), jnp.zeros((n, dq - rot), F32)
    tab_c = jnp.tile(jnp.concatenate([cos, cos, one], axis=1), (1, LANES // dq))
    tab_s = jnp.tile(jnp.concatenate([sin, sin, zero], axis=1), (1, LANES // dq))
    return tab_c, tab_s


def _inproj_kernel(xt_ref, xb_ref, g_ref, scale_ref, shift_ref, tc_ref, ts_ref, w_ref, o_ref,
                   h_ref, *, dq, rot, q_scale, n_rope_blocks, chunk, norm_rows_per_iter):
    j = pl.program_id(1)

    @pl.when(j == 0)
    def _():
        gs = g_ref[...] * (1.0 + scale_ref[0])
        sh = shift_ref[0]
        half = xt_ref.shape[0]
        for part, x_ref in enumerate((xt_ref, xb_ref)):

            def norm_rows(r, carry, part=part, x_ref=x_ref):
                r0 = pl.multiple_of(r * norm_rows_per_iter, norm_rows_per_iter)
                x = x_ref[pl.ds(r0, norm_rows_per_iter), :]
                ms = jnp.mean(x * x, axis=-1, keepdims=True)
                h_ref[pl.ds(part * half + r0, norm_rows_per_iter), :] = (
                    x * lax.rsqrt(ms + EPS) * gs + sh).astype(BF16)
                return carry

            lax.fori_loop(0, half // norm_rows_per_iter, norm_rows, 0, unroll=8)

    @pl.when(j < n_rope_blocks)
    def _():
        sc = jnp.where(j < n_rope_blocks // 2, q_scale, 1.0).astype(F32)
        cs, sn = tc_ref[...] * sc, ts_ref[...] * sc
        d = lax.broadcasted_iota(jnp.int32, sn.shape, 1) % dq
        su = jnp.where(d < rot // 2, sn, 0.0)
        sd = jnp.where(d >= rot // 2, sn, 0.0)
        for c in range(w_ref.shape[1] // chunk):
            acc = jnp.dot(h_ref[...], w_ref[:, c * chunk:(c + 1) * chunk].astype(BF16),
                          preferred_element_type=F32)
            for t in range(chunk // LANES):
                a = acc[:, t * LANES:(t + 1) * LANES]
                up = pltpu.roll(a, LANES - rot // 2, 1)
                dn = pltpu.roll(a, rot // 2, 1)
                col = c * chunk + t * LANES
                o_ref[:, col:col + LANES] = (a * cs - up * su + dn * sd).astype(o_ref.dtype)

    @pl.when(j >= n_rope_blocks)
    def _():
        o_ref[...] = jnp.dot(h_ref[...], w_ref[...].astype(BF16),
                             preferred_element_type=F32).astype(o_ref.dtype)


def _in_proj(x2, norm_g, mod3, tabs, w_in, *, seq, dq, rot, d_qk, tm=1024, tn=1024):
    m, d = x2.shape
    n = w_in.shape[1]
    blocks_per_seq = seq // tm
    kern = functools.partial(_inproj_kernel, dq=dq, rot=rot, q_scale=LOG2E / math.sqrt(dq),
                             n_rope_blocks=d_qk // tn, chunk=256, norm_rows_per_iter=16)
    tab_spec = pl.BlockSpec((tm, LANES), lambda i, j: (i, 0))
    n_rows, n_cols = m // tm, n // tn

    def x_half(part):
        switch = n_cols - 2 + part
        return pl.BlockSpec((tm // 2, d), lambda i, j: (
            2 * jnp.where(j >= switch, jnp.minimum(i + 1, n_rows - 1), i) + part, 0))

    return pl.pallas_call(
        kern,
        grid=(n_rows, n_cols),
        in_specs=[
            x_half(0), x_half(1),
            pl.BlockSpec((1, d), lambda i, j: (0, 0)),
            pl.BlockSpec((1, 1, d), lambda i, j: ((i // blocks_per_seq) * 3 + 1, 0, 0)),
            pl.BlockSpec((1, 1, d), lambda i, j: ((i // blocks_per_seq) * 3 + 0, 0, 0)),
            tab_spec, tab_spec,
            pl.BlockSpec((d, tn), lambda i, j: (0, j)),
        ],
        out_specs=pl.BlockSpec((tm, tn), lambda i, j: (i, j)),
        out_shape=jax.ShapeDtypeStruct((m, n), BF16),
        scratch_shapes=[pltpu.VMEM((tm, d), BF16)],
        compiler_params=pltpu.CompilerParams(
            dimension_semantics=("arbitrary", "arbitrary"), vmem_limit_bytes=VMEM_LIMIT),
        name="in_proj",
    )(x2, x2, norm_g, mod3, mod3, *tabs, w_in)


def _attn_kernel(lam_ref, sg_ref, q_ref, k_ref, v_ref, g_ref, o_ref,
                 vt_ref, qq_ref, s_ref, p_ref, al_ref, m_ref, acc_ref, *, qb, kb, dq, row_chunk, lam_init):
    seq, dv = v_ref.shape
    nq = seq // qb

    vt_ref[0:dv, :] = v_ref[...].astype(F32).T.astype(BF16)
    vt_ref[dv:, :] = jnp.ones((vt_ref.shape[0] - dv, seq), BF16)

    lam = (jnp.exp(jnp.sum(lam_ref[0:1, :] * lam_ref[1:2, :], axis=-1, keepdims=True))
           - jnp.exp(jnp.sum(lam_ref[2:3, :] * lam_ref[3:4, :], axis=-1, keepdims=True))
           + lam_init)

    def build_qq(qi):
        q = q_ref[pl.ds(_aligned(qi * qb, qb), qb), :]
        lane = lax.broadcasted_iota(jnp.int32, q.shape, 1)
        zero = jnp.zeros_like(q)
        qq_ref[0:qb, :] = jnp.where(lane < dq, q, zero)
        qq_ref[qb:, :] = jnp.where(lane >= dq, q, zero)

    def scores(j, slot):
        k0 = _aligned(j * kb, kb)
        s = lax.dot_general(k_ref[pl.ds(k0, kb), :], qq_ref[...], (((1,), (1,)), ((), ())),
                            preferred_element_type=F32)
        s_ref[slot] = s

    def softmax(slot, masked):
        for c in range(2):
            cols = slice(c * qb, (c + 1) * qb)

            def chunk(r):
                s = s_ref[slot, r:r + row_chunk, cols]
                if masked:
                    kpos = r + lax.broadcasted_iota(jnp.int32, s.shape, 0)
                    qpos = lax.broadcasted_iota(jnp.int32, s.shape, 1)
                    s = jnp.where(kpos <= qpos, s, NEG_BIG)
                return s

            part = None
            for r in range(0, kb, row_chunk):
                t = jnp.max(chunk(r).reshape(row_chunk // SUBLANES, SUBLANES, qb), axis=0)
                part = t if part is None else jnp.maximum(part, t)
            m_old = m_ref[:, cols]
            m_new = jnp.maximum(m_old, jnp.max(part, axis=0, keepdims=True))
            al_ref[slot, :, cols] = jnp.exp2(m_old - m_new)
            m_ref[:, cols] = m_new
            for r in range(0, kb, row_chunk):
                p_ref[slot, r:r + row_chunk, cols] = jnp.exp2(chunk(r) - m_new).astype(BF16)

    def pv_update(j, slot):
        k0 = _aligned(j * kb, kb)
        pv = jnp.dot(vt_ref[:, pl.ds(k0, kb)], p_ref[slot], preferred_element_type=F32)
        acc_ref[...] = al_ref[slot] * acc_ref[...] + pv

    def finalize(qi):
        rows = pl.ds(_aligned(qi * qb, qb), qb)
        o_all = acc_ref[0:dv, :] / acc_ref[dv:dv + 1, :]
        o = o_all[:, :qb] - lam * o_all[:, qb:]
        ms = jnp.mean(o * o, axis=0, keepdims=True)
        on = (o * lax.rsqrt(ms + EPS)).T
        on = on * sg_ref[...] * (1.0 - lam_init)
        o_ref[rows, :] = (on * _silu(g_ref[rows, :].astype(F32))).astype(o_ref.dtype)

    pairs = [(qi, j) for qi in range(nq) for j in range(qi + 1)]
    build_qq(0)
    m_ref[...] = jnp.full(m_ref.shape, NEG_BIG, F32)
    acc_ref[...] = jnp.zeros(acc_ref.shape, F32)
    scores(0, 0)
    for n, (qi, j) in enumerate(pairs):
        slot, prev, nxt = n % N_SLOTS, (n - 1) % N_SLOTS, (n + 1) % N_SLOTS
        if j == qi and qi + 1 < nq:
            build_qq(qi + 1)
        softmax(slot, masked=(j == qi))
        if n > 0:
            pv_update(pairs[n - 1][1], prev)
        if n + 1 < len(pairs):
            scores(pairs[n + 1][1], nxt)
        if j == 0 and qi > 0:
            finalize(qi - 1)
            acc_ref[...] = jnp.zeros(acc_ref.shape, F32)
        if j == qi:
            m_ref[...] = jnp.full(m_ref.shape, NEG_BIG, F32)
    pv_update(pairs[-1][1], (len(pairs) - 1) % N_SLOTS)
    finalize(nq - 1)


def _diff_attn(z, lam_params, subln_g, *, batch, seq, dq, dv, lam_init, qb=512):
    kb = qb
    h = N_HEADS
    k_col0 = h * 2 * dq // LANES
    v_col0 = 2 * k_col0
    g_col0 = v_col0 + h * dv // LANES
    kern = functools.partial(_attn_kernel, qb=qb, kb=kb, dq=dq, row_chunk=32, lam_init=lam_init)
    return pl.pallas_call(
        kern,
        grid=(batch, h),
        in_specs=[
            pl.BlockSpec(lam_params.shape, lambda b, hh: (0, 0)),
            pl.BlockSpec((1, dv), lambda b, hh: (0, 0)),
            pl.BlockSpec((seq, 2 * dq), lambda b, hh: (b, hh)),
            pl.BlockSpec((seq, 2 * dq), lambda b, hh: (b, k_col0 + hh)),
            pl.BlockSpec((seq, dv), lambda b, hh: (b, v_col0 + hh)),
            pl.BlockSpec((seq, dv), lambda b, hh: (b, g_col0 + hh)),
        ],
        out_specs=pl.BlockSpec((seq, dv), lambda b, hh: (b, hh)),
        out_shape=jax.ShapeDtypeStruct((batch * seq, h * dv), BF16),
        scratch_shapes=[pltpu.VMEM((dv + ONES_ROWS, seq), BF16),
                        pltpu.VMEM((2 * qb, 2 * dq), BF16),
                        pltpu.VMEM((N_SLOTS, kb, 2 * qb), F32),
                        pltpu.VMEM((N_SLOTS, kb, 2 * qb), BF16),
                        pltpu.VMEM((N_SLOTS, 1, 2 * qb), F32),
                        pltpu.VMEM((1, 2 * qb), F32),
                        pltpu.VMEM((dv + ONES_ROWS, 2 * qb), F32)],
        compiler_params=pltpu.CompilerParams(
            dimension_semantics=("arbitrary", "arbitrary"),
            vmem_limit_bytes=VMEM_LIMIT),
        name="diff_attn",
    )(lam_params, subln_g, z, z, z, z)


def _conv_kernel(ua_ref, ub_ref, ha_ref, hb_ref, gc_ref, dww_ref, dwb_ref, lng_ref, lnb_ref,
                 wpw_ref, bpw_ref, o_ref, sh_ref, wb_ref, conv_ref, *, ts, row_chunk, lane_chunk):
    i = pl.program_id(1)
    c = ua_ref.shape[1]

    @pl.when((pl.program_id(0) == 0) & (i == 0))
    def _():
        for j in range(CONV_WIDTH):
            wb_ref[j] = jnp.broadcast_to(dww_ref[j:j + 1, :], (SUBLANES, c))

    def glu(a_ref, b_ref):
        return a_ref[...].astype(F32) * jax.nn.sigmoid(b_ref[...].astype(F32))

    hist = jnp.where(i > 0, glu(ha_ref, hb_ref), 0.0)
    ypad = jnp.concatenate([hist, glu(ua_ref, ub_ref)], axis=0)
    for r in range(SUBLANES):
        n_r = HALO + ts - (SUBLANES if r else 0)
        sh_ref[r, 0:n_r, 0:c] = ypad[r:r + n_r, :]

    base = HALO - (CONV_WIDTH - 1)
    n_rc = ts // row_chunk
    for lc in range(c // lane_chunk):
        l0 = lc * lane_chunk

        def body(rc, carry, l0=l0):
            r0 = pl.multiple_of(rc * row_chunk, row_chunk)
            parts = []
            for r in range(SUBLANES):
                taps = [j for j in range(CONV_WIDTH) if (base + j) % SUBLANES == r]
                a_lo, a_hi = (base + taps[0]) // SUBLANES, (base + taps[-1]) // SUBLANES
                n_win = row_chunk + (a_hi - a_lo) * SUBLANES
                win = sh_ref[r, pl.ds(r0 + a_lo * SUBLANES, n_win), l0:l0 + lane_chunk]
                win = win.reshape(n_win // SUBLANES, SUBLANES, lane_chunk)
                part = None
                for j in taps:
                    t0 = (base + j) // SUBLANES - a_lo
                    term = win[t0:t0 + row_chunk // SUBLANES] * wb_ref[j, :, l0:l0 + lane_chunk]
                    part = term if part is None else part + term
                parts.append(part)
                if len(parts) == 3:
                    parts = [parts[0] + (parts[1] + parts[2])]
            acc = parts[0] if len(parts) == 1 else parts[0] + parts[1]
            conv_ref[pl.ds(r0, row_chunk), l0:l0 + lane_chunk] = acc.reshape(row_chunk, lane_chunk)
            return carry

        lax.fori_loop(0, n_rc, body, 0)

    y = conv_ref[...] + dwb_ref[...]
    mu = jnp.mean(y, axis=-1, keepdims=True)
    var = jnp.mean(jnp.square(y - mu), axis=-1, keepdims=True)
    y = (y - mu) * lax.rsqrt(var + EPS) * lng_ref[...] + lnb_ref[...]
    y = _silu(y)
    y = jnp.dot(y.astype(BF16), wpw_ref[...], preferred_element_type=F32) + bpw_ref[...]
    o_ref[...] = (y * _silu(gc_ref[...].astype(F32))).astype(o_ref.dtype)


def _conv_mod(z, dw_w, dw_b, ln_g, ln_b, w_pw_bf, b_pw, *, batch, seq, c, ua_col0, ts=512):
    nt = seq // ts
    ua_blk, ub_blk, gc_blk = ua_col0 // c, ua_col0 // c + 1, ua_col0 // c + 2
    halo_per_ts = ts // HALO
    kern = functools.partial(_conv_kernel, ts=ts, row_chunk=64, lane_chunk=128)

    def halo_map(blk):
        return lambda b, i: (jnp.maximum((b * nt + i) * halo_per_ts - 1, 0), blk)

    row = lambda b, i: (0, 0)
    return pl.pallas_call(
        kern,
        grid=(batch, nt),
        in_specs=[
            pl.BlockSpec((ts, c), lambda b, i: (b * nt + i, ua_blk)),
            pl.BlockSpec((ts, c), lambda b, i: (b * nt + i, ub_blk)),
            pl.BlockSpec((HALO, c), halo_map(ua_blk)),
            pl.BlockSpec((HALO, c), halo_map(ub_blk)),
            pl.BlockSpec((ts, c), lambda b, i: (b * nt + i, gc_blk)),
            pl.BlockSpec((CONV_WIDTH, c), row),
            pl.BlockSpec((1, c), row),
            pl.BlockSpec((1, c), row),
            pl.BlockSpec((1, c), row),
            pl.BlockSpec((c, c), row),
            pl.BlockSpec((1, c), row),
        ],
        out_specs=pl.BlockSpec((ts, c), lambda b, i: (b * nt + i, 0)),
        out_shape=jax.ShapeDtypeStruct((batch * seq, c), BF16),
        scratch_shapes=[pltpu.VMEM((SUBLANES, HALO + ts, c + LANES), F32),
                        pltpu.VMEM((CONV_WIDTH, SUBLANES, c), F32),
                        pltpu.VMEM((ts, c), F32)],
        compiler_params=pltpu.CompilerParams(
            dimension_semantics=("arbitrary", "arbitrary"), vmem_limit_bytes=VMEM_LIMIT),
        name="conv_mod",
    )(z, z, z, z, z, dw_w, dw_b, ln_g, ln_b, w_pw_bf, b_pw)


def _outproj_kernel(ya_ref, yc_ref, wa_ref, wc_ref, x_ref, gate_ref, fg_ref, o_ref, *, final_norm):
    mixed = jnp.dot(ya_ref[...], wa_ref[...].astype(BF16), preferred_element_type=F32)
    mixed = mixed + jnp.dot(yc_ref[...], wc_ref[...].astype(BF16), preferred_element_type=F32)
    xn = x_ref[...] + gate_ref[0] * mixed
    if final_norm:
        ms = jnp.mean(xn * xn, axis=-1, keepdims=True)
        xn = xn * lax.rsqrt(ms + EPS) * fg_ref[...]
    o_ref[...] = xn


def _out_proj(y_attn, y_conv, w_out, x2, mod3, final_g, *, seq, final_norm, tm=512):
    m, d = x2.shape
    da, dc = y_attn.shape[1], y_conv.shape[1]
    blocks_per_seq = seq // tm
    return pl.pallas_call(
        functools.partial(_outproj_kernel, final_norm=final_norm),
        grid=(m // tm,),
        in_specs=[
            pl.BlockSpec((tm, da), lambda i: (i, 0)),
            pl.BlockSpec((tm, dc), lambda i: (i, 0)),
            pl.BlockSpec((da, d), lambda i: (0, 0)),
            pl.BlockSpec((dc, d), lambda i: (da // dc, 0)),
            pl.BlockSpec((tm, d), lambda i: (i, 0)),
            pl.BlockSpec((1, 1, d), lambda i: ((i // blocks_per_seq) * 3 + 2, 0, 0)),
            pl.BlockSpec((1, d), lambda i: (0, 0)),
        ],
        out_specs=pl.BlockSpec((tm, d), lambda i: (i, 0)),
        out_shape=jax.ShapeDtypeStruct((m, d), F32),
        compiler_params=pltpu.CompilerParams(
            dimension_semantics=("arbitrary",), vmem_limit_bytes=VMEM_LIMIT),
        name="out_proj",
    )(y_attn, y_conv, w_out, w_out, x2, mod3, final_g)


def kernel(x, c, positions, norm_g, w_ada, b_ada, w_in, lambda_q1, lambda_k1, lambda_q2, lambda_k2,
           subln_g, conv_dw_w, conv_dw_b, conv_ln_g, conv_ln_b, w_pw, b_pw, w_out, final_g):
    batch, seq, d = x.shape
    depth = w_in.shape[0]
    d_attn = d // 2
    d_conv = d - d_attn
    dv = d_attn // N_HEADS
    dq = dv // 2
    rot = dq // 4

    tabs = _rope_tables(positions, dq=dq, rot=rot)
    c_pad = jnp.zeros((SUBLANES, d), F32).at[:batch].set(c)

    x2 = x.reshape(batch * seq, d)
    for l in range(depth):
        lam_init = 0.8 - 0.6 * math.exp(-0.3 * l)
        mod = _ada_mod(c_pad, w_ada[l], b_ada[l].reshape(1, -1))
        mod3 = mod[:batch].reshape(batch * 3, 1, d)
        z = _in_proj(x2, norm_g[l].reshape(1, d), mod3, tabs, w_in[l],
                     seq=seq, dq=dq, rot=rot, d_qk=2 * d_attn)
        lam_params = jnp.stack([lambda_q1[l], lambda_k1[l], lambda_q2[l], lambda_k2[l]]).astype(F32)
        y_attn = _diff_attn(z, lam_params, subln_g[l].reshape(1, dv),
                            batch=batch, seq=seq, dq=dq, dv=dv, lam_init=lam_init)
        y_conv = _conv_mod(z, conv_dw_w[l], conv_dw_b[l].reshape(1, -1), conv_ln_g[l].reshape(1, -1),
                           conv_ln_b[l].reshape(1, -1), w_pw[l].astype(BF16), b_pw[l].reshape(1, -1),
                           batch=batch, seq=seq, c=d_conv, ua_col0=4 * d_attn)
        x2 = _out_proj(y_attn, y_conv, w_out[l], x2, mod3, final_g.reshape(1, d),
                       seq=seq, final_norm=(l == depth - 1))
    return x2.reshape(batch, seq, d)
```

```python
import functools
import math

import jax
import jax.numpy as jnp
from jax import lax
from jax.experimental import pallas as pl
from jax.experimental.pallas import tpu as pltpu

F32 = jnp.float32
BF16 = jnp.bfloat16

N_HEADS = 8
CONV_WIDTH = 31
ROPE_THETA = 500000.0
EPS = 1e-6
LANES = 128
SUBLANES = 8
HALO = 32
NEG_BIG = -1e30
ONES_ROWS = 16
LOG2E = math.log2(math.e)
N_SLOTS = 3
VMEM_LIMIT = 56 * 1024 * 1024


def _aligned(start, multiple):
    return start if isinstance(start, int) else pl.multiple_of(start, multiple)


def _silu(t):
    return t * jax.nn.sigmoid(t)


def _ada_kernel(c_ref, w_ref, b_ref, o_ref):
    ca = _silu(c_ref[...])
    o_ref[...] = jnp.dot(ca.astype(BF16), w_ref[...].astype(BF16),
                         preferred_element_type=F32) + b_ref[...]


def _ada_mod(c_pad, w_ada, b_ada, tn=1024):
    rows, d = c_pad.shape
    n = w_ada.shape[1]
    return pl.pallas_call(
        _ada_kernel,
        grid=(n // tn,),
        in_specs=[pl.BlockSpec((rows, d), lambda j: (0, 0)),
                  pl.BlockSpec((d, tn), lambda j: (0, j)),
                  pl.BlockSpec((1, tn), lambda j: (0, j))],
        out_specs=pl.BlockSpec((rows, tn), lambda j: (0, j)),
        out_shape=jax.ShapeDtypeStruct((rows, n), F32),
        compiler_params=pltpu.CompilerParams(
            dimension_semantics=("arbitrary",), vmem_limit_bytes=VMEM_LIMIT),
        name="ada_mod",
    )(c_pad, w_ada, b_ada)


def _rope_tab_kernel(pos_ref, freq_ref, cos_ref, sin_ref):
    ang = pos_ref[...].astype(F32) * freq_ref[...]
    cos_ref[...] = jnp.cos(ang)
    sin_ref[...] = jnp.sin(ang)


def _rope_tables(positions, *, dq, rot):
    n = positions.size
    nf = rot // 2
    inv_freq = ROPE_THETA ** (-jnp.arange(0, rot, 2, dtype=F32) / rot)
    freq_lane = jnp.tile(inv_freq, LANES // nf).reshape(1, LANES)
    pos_rep = jnp.repeat(positions.reshape(-1), nf).reshape(n * nf // LANES, LANES)
    cos, sin = pl.pallas_call(
        _rope_tab_kernel,
        out_shape=[jax.ShapeDtypeStruct(pos_rep.shape, F32)] * 2,
        name="rope_tab",
    )(pos_rep, freq_lane)
    cos, sin = cos.reshape(n, nf), sin.reshape(n, nf)
    one, zero = jnp.ones((n, dq - rot), F32), jnp.zeros((n, dq - rot), F32)
    tab_c = jnp.tile(jnp.concatenate([cos, cos, one], axis=1), (1, LANES // dq))
    tab_s = jnp.tile(jnp.concatenate([sin, sin, zero], axis=1), (1, LANES // dq))
    return tab_c, tab_s


def _inproj_kernel(xt_ref, xb_ref, g_ref, scale_ref, shift_ref, tc_ref, ts_ref, w_ref, o_ref,
                   h_ref, *, dq, rot, q_scale, n_rope_blocks, chunk, norm_rows_per_iter):
    j = pl.program_id(1)

    @pl.when(j == 0)
    def _():
        gs = g_ref[...] * (1.0 + scale_ref[0])
        sh = shift_ref[0]
        half = xt_ref.shape[0]
        for part, x_ref in enumerate((xt_ref, xb_ref)):

            def norm_rows(r, carry, part=part, x_ref=x_ref):
                r0 = pl.multiple_of(r * norm_rows_per_iter, norm_rows_per_iter)
                x = x_ref[pl.ds(r0, norm_rows_per_iter), :]
                ms = jnp.mean(x * x, axis=-1, keepdims=True)
                h_ref[pl.ds(part * half + r0, norm_rows_per_iter), :] = (
                    x * lax.rsqrt(ms + EPS) * gs + sh).astype(BF16)
                return carry

            lax.fori_loop(0, half // norm_rows_per_iter, norm_rows, 0, unroll=8)

    @pl.when(j < n_rope_blocks)
    def _():
        sc = jnp.where(j < n_rope_blocks // 2, q_scale, 1.0).astype(F32)
        cs, sn = tc_ref[...] * sc, ts_ref[...] * sc
        d = lax.broadcasted_iota(jnp.int32, sn.shape, 1) % dq
        su = jnp.where(d < rot // 2, sn, 0.0)
        sd = jnp.where(d >= rot // 2, sn, 0.0)
        for c in range(w_ref.shape[1] // chunk):
            acc = jnp.dot(h_ref[...], w_ref[:, c * chunk:(c + 1) * chunk].astype(BF16),
                          preferred_element_type=F32)
            for t in range(chunk // LANES):
                a = acc[:, t * LANES:(t + 1) * LANES]
                up = pltpu.roll(a, LANES - rot // 2, 1)
                dn = pltpu.roll(a, rot // 2, 1)
                col = c * chunk + t * LANES
                o_ref[:, col:col + LANES] = (a * cs - up * su + dn * sd).astype(o_ref.dtype)

    @pl.when(j >= n_rope_blocks)
    def _():
        o_ref[...] = jnp.dot(h_ref[...], w_ref[...].astype(BF16),
                             preferred_element_type=F32).astype(o_ref.dtype)


def _in_proj(x2, norm_g, mod3, tabs, w_in, *, seq, dq, rot, d_qk, tm=1024, tn=1024):
    m, d = x2.shape
    n = w_in.shape[1]
    blocks_per_seq = seq // tm
    kern = functools.partial(_inproj_kernel, dq=dq, rot=rot, q_scale=LOG2E / math.sqrt(dq),
                             n_rope_blocks=d_qk // tn, chunk=256, norm_rows_per_iter=16)
    tab_spec = pl.BlockSpec((tm, LANES), lambda i, j: (i, 0))
    n_rows, n_cols = m // tm, n // tn

    def x_half(part):
        switch = n_cols - 2 + part
        return pl.BlockSpec((tm // 2, d), lambda i, j: (
            2 * jnp.where(j >= switch, jnp.minimum(i + 1, n_rows - 1), i) + part, 0))

    return pl.pallas_call(
        kern,
        grid=(n_rows, n_cols),
        in_specs=[
            x_half(0), x_half(1),
            pl.BlockSpec((1, d), lambda i, j: (0, 0)),
            pl.BlockSpec((1, 1, d), lambda i, j: ((i // blocks_per_seq) * 3 + 1, 0, 0)),
            pl.BlockSpec((1, 1, d), lambda i, j: ((i // blocks_per_seq) * 3 + 0, 0, 0)),
            tab_spec, tab_spec,
            pl.BlockSpec((d, tn), lambda i, j: (0, j)),
        ],
        out_specs=pl.BlockSpec((tm, tn), lambda i, j: (i, j)),
        out_shape=jax.ShapeDtypeStruct((m, n), BF16),
        scratch_shapes=[pltpu.VMEM((tm, d), BF16)],
        compiler_params=pltpu.CompilerParams(
            dimension_semantics=("arbitrary", "arbitrary"), vmem_limit_bytes=VMEM_LIMIT),
        name="in_proj",
    )(x2, x2, norm_g, mod3, mod3, *tabs, w_in)


def _attn_kernel(lq1_ref, lk1_ref, lq2_ref, lk2_ref, sg_ref, q_ref, k_ref, v_ref, g_ref, o_ref,
                 vt_ref, qq_ref, s_ref, p_ref, al_ref, m_ref, acc_ref, *, qb, kb, dq, row_chunk, lam_init):
    seq, dv = v_ref.shape
    nq = seq // qb

    vt_ref[0:dv, :] = v_ref[...].astype(F32).T.astype(BF16)
    vt_ref[dv:, :] = jnp.ones((vt_ref.shape[0] - dv, seq), BF16)

    lam = (jnp.exp(jnp.sum(lq1_ref[...] * lk1_ref[...], axis=-1, keepdims=True))
           - jnp.exp(jnp.sum(lq2_ref[...] * lk2_ref[...], axis=-1, keepdims=True))
           + lam_init)

    def build_qq(qi):
        q = q_ref[pl.ds(_aligned(qi * qb, qb), qb), :]
        lane = lax.broadcasted_iota(jnp.int32, q.shape, 1)
        zero = jnp.zeros_like(q)
        qq_ref[0:qb, :] = jnp.where(lane < dq, q, zero)
        qq_ref[qb:, :] = jnp.where(lane >= dq, q, zero)

    def scores(j, slot):
        k0 = _aligned(j * kb, kb)
        s = lax.dot_general(k_ref[pl.ds(k0, kb), :], qq_ref[...], (((1,), (1,)), ((), ())),
                            preferred_element_type=F32)
        s_ref[slot] = s

    def softmax(slot, masked):
        for c in range(2):
            cols = slice(c * qb, (c + 1) * qb)

            def chunk(r):
                s = s_ref[slot, r:r + row_chunk, cols]
                if masked:
                    kpos = r + lax.broadcasted_iota(jnp.int32, s.shape, 0)
                    qpos = lax.broadcasted_iota(jnp.int32, s.shape, 1)
                    s = jnp.where(kpos <= qpos, s, NEG_BIG)
                return s

            part = None
            for r in range(0, kb, row_chunk):
                t = jnp.max(chunk(r).reshape(row_chunk // SUBLANES, SUBLANES, qb), axis=0)
                part = t if part is None else jnp.maximum(part, t)
            m_old = m_ref[:, cols]
            m_new = jnp.maximum(m_old, jnp.max(part, axis=0, keepdims=True))
            al_ref[slot, :, cols] = jnp.exp2(m_old - m_new)
            m_ref[:, cols] = m_new
            for r in range(0, kb, row_chunk):
                p_ref[slot, r:r + row_chunk, cols] = jnp.exp2(chunk(r) - m_new).astype(BF16)

    def pv_update(j, slot):
        k0 = _aligned(j * kb, kb)
        pv = jnp.dot(vt_ref[:, pl.ds(k0, kb)], p_ref[slot], preferred_element_type=F32)
        acc_ref[...] = al_ref[slot] * acc_ref[...] + pv

    def finalize(qi):
        rows = pl.ds(_aligned(qi * qb, qb), qb)
        o_all = acc_ref[0:dv, :] / acc_ref[dv:dv + 1, :]
        o = o_all[:, :qb] - lam * o_all[:, qb:]
        ms = jnp.mean(o * o, axis=0, keepdims=True)
        on = (o * lax.rsqrt(ms + EPS)).T
        on = on * sg_ref[...] * (1.0 - lam_init)
        o_ref[rows, :] = (on * _silu(g_ref[rows, :].astype(F32))).astype(o_ref.dtype)

    pairs = [(qi, j) for qi in range(nq) for j in range(qi + 1)]
    build_qq(0)
    m_ref[...] = jnp.full(m_ref.shape, NEG_BIG, F32)
    acc_ref[...] = jnp.zeros(acc_ref.shape, F32)
    scores(0, 0)
    for n, (qi, j) in enumerate(pairs):
        slot, prev, nxt = n % N_SLOTS, (n - 1) % N_SLOTS, (n + 1) % N_SLOTS
        if j == qi and qi + 1 < nq:
            build_qq(qi + 1)
        softmax(slot, masked=(j == qi))
        if n > 0:
            pv_update(pairs[n - 1][1], prev)
        if n + 1 < len(pairs):
            scores(pairs[n + 1][1], nxt)
        if j == 0 and qi > 0:
            finalize(qi - 1)
            acc_ref[...] = jnp.zeros(acc_ref.shape, F32)
        if j == qi:
            m_ref[...] = jnp.full(m_ref.shape, NEG_BIG, F32)
    pv_update(pairs[-1][1], (len(pairs) - 1) % N_SLOTS)
    finalize(nq - 1)


def _diff_attn(z, lam_vecs, subln_g, *, batch, seq, dq, dv, lam_init, qb=512):
    kb = qb
    h = N_HEADS
    k_col0 = h * 2 * dq // LANES
    v_col0 = 2 * k_col0
    g_col0 = v_col0 + h * dv // LANES
    kern = functools.partial(_attn_kernel, qb=qb, kb=kb, dq=dq, row_chunk=32, lam_init=lam_init)
    return pl.pallas_call(
        kern,
        grid=(batch, h),
        in_specs=[
            *[pl.BlockSpec((1, dq), lambda b, hh: (0, 0))] * 4,
            pl.BlockSpec((1, dv), lambda b, hh: (0, 0)),
            pl.BlockSpec((seq, 2 * dq), lambda b, hh: (b, hh)),
            pl.BlockSpec((seq, 2 * dq), lambda b, hh: (b, k_col0 + hh)),
            pl.BlockSpec((seq, dv), lambda b, hh: (b, v_col0 + hh)),
            pl.BlockSpec((seq, dv), lambda b, hh: (b, g_col0 + hh)),
        ],
        out_specs=pl.BlockSpec((seq, dv), lambda b, hh: (b, hh)),
        out_shape=jax.ShapeDtypeStruct((batch * seq, h * dv), BF16),
        scratch_shapes=[pltpu.VMEM((dv + ONES_ROWS, seq), BF16),
                        pltpu.VMEM((2 * qb, 2 * dq), BF16),
                        pltpu.VMEM((N_SLOTS, kb, 2 * qb), F32),
                        pltpu.VMEM((N_SLOTS, kb, 2 * qb), BF16),
                        pltpu.VMEM((N_SLOTS, 1, 2 * qb), F32),
                        pltpu.VMEM((1, 2 * qb), F32),
                        pltpu.VMEM((dv + ONES_ROWS, 2 * qb), F32)],
        compiler_params=pltpu.CompilerParams(
            dimension_semantics=("arbitrary", "arbitrary"),
            vmem_limit_bytes=VMEM_LIMIT),
        name="diff_attn",
    )(*lam_vecs, subln_g, z, z, z, z)


def _conv_kernel(ua_ref, ub_ref, ha_ref, hb_ref, gc_ref, dww_ref, dwb_ref, lng_ref, lnb_ref,
                 wpw_ref, bpw_ref, o_ref, sh_ref, wb_ref, conv_ref, *, ts, row_chunk, lane_chunk):
    i = pl.program_id(1)
    c = ua_ref.shape[1]

    @pl.when((pl.program_id(0) == 0) & (i == 0))
    def _():
        for j in range(CONV_WIDTH):
            wb_ref[j] = jnp.broadcast_to(dww_ref[j:j + 1, :], (SUBLANES, c))

    def glu(a_ref, b_ref):
        return a_ref[...].astype(F32) * jax.nn.sigmoid(b_ref[...].astype(F32))

    hist = jnp.where(i > 0, glu(ha_ref, hb_ref), 0.0)
    ypad = jnp.concatenate([hist, glu(ua_ref, ub_ref)], axis=0)
    for r in range(SUBLANES):
        n_r = HALO + ts - (SUBLANES if r else 0)
        sh_ref[r, 0:n_r, 0:c] = ypad[r:r + n_r, :]

    base = HALO - (CONV_WIDTH - 1)
    n_rc = ts // row_chunk
    for lc in range(c // lane_chunk):
        l0 = lc * lane_chunk

        def body(rc, carry, l0=l0):
            r0 = pl.multiple_of(rc * row_chunk, row_chunk)
            parts = []
            for r in range(SUBLANES):
                taps = [j for j in range(CONV_WIDTH) if (base + j) % SUBLANES == r]
                a_lo, a_hi = (base + taps[0]) // SUBLANES, (base + taps[-1]) // SUBLANES
                n_win = row_chunk + (a_hi - a_lo) * SUBLANES
                win = sh_ref[r, pl.ds(r0 + a_lo * SUBLANES, n_win), l0:l0 + lane_chunk]
                win = win.reshape(n_win // SUBLANES, SUBLANES, lane_chunk)
                part = None
                for j in taps:
                    t0 = (base + j) // SUBLANES - a_lo
                    term = win[t0:t0 + row_chunk // SUBLANES] * wb_ref[j, :, l0:l0 + lane_chunk]
                    part = term if part is None else part + term
                parts.append(part)
                if len(parts) == 3:
                    parts = [parts[0] + (parts[1] + parts[2])]
            acc = parts[0] if len(parts) == 1 else parts[0] + parts[1]
            conv_ref[pl.ds(r0, row_chunk), l0:l0 + lane_chunk] = acc.reshape(row_chunk, lane_chunk)
            return carry

        lax.fori_loop(0, n_rc, body, 0)

    y = conv_ref[...] + dwb_ref[...]
    mu = jnp.mean(y, axis=-1, keepdims=True)
    var = jnp.mean(jnp.square(y - mu), axis=-1, keepdims=True)
    y = (y - mu) * lax.rsqrt(var + EPS) * lng_ref[...] + lnb_ref[...]
    y = _silu(y)
    y = jnp.dot(y.astype(BF16), wpw_ref[...].astype(BF16), preferred_element_type=F32) + bpw_ref[...]
    o_ref[...] = (y * _silu(gc_ref[...].astype(F32))).astype(o_ref.dtype)


def _conv_mod(z, dw_w, dw_b, ln_g, ln_b, w_pw, b_pw, *, batch, seq, c, ua_col0, ts=512):
    nt = seq // ts
    ua_blk, ub_blk, gc_blk = ua_col0 // c, ua_col0 // c + 1, ua_col0 // c + 2
    halo_per_ts = ts // HALO
    kern = functools.partial(_conv_kernel, ts=ts, row_chunk=64, lane_chunk=128)

    def halo_map(blk):
        return lambda b, i: (jnp.maximum((b * nt + i) * halo_per_ts - 1, 0), blk)

    row = lambda b, i: (0, 0)
    return pl.pallas_call(
        kern,
        grid=(batch, nt),
        in_specs=[
            pl.BlockSpec((ts, c), lambda b, i: (b * nt + i, ua_blk)),
            pl.BlockSpec((ts, c), lambda b, i: (b * nt + i, ub_blk)),
            pl.BlockSpec((HALO, c), halo_map(ua_blk)),
            pl.BlockSpec((HALO, c), halo_map(ub_blk)),
            pl.BlockSpec((ts, c), lambda b, i: (b * nt + i, gc_blk)),
            pl.BlockSpec((CONV_WIDTH, c), row),
            pl.BlockSpec((1, c), row),
            pl.BlockSpec((1, c), row),
            pl.BlockSpec((1, c), row),
            pl.BlockSpec((c, c), row),
            pl.BlockSpec((1, c), row),
        ],
        out_specs=pl.BlockSpec((ts, c), lambda b, i: (b * nt + i, 0)),
        out_shape=jax.ShapeDtypeStruct((batch * seq, c), BF16),
        scratch_shapes=[pltpu.VMEM((SUBLANES, HALO + ts, c + LANES), F32),
                        pltpu.VMEM((CONV_WIDTH, SUBLANES, c), F32),
                        pltpu.VMEM((ts, c), F32)],
        compiler_params=pltpu.CompilerParams(
            dimension_semantics=("arbitrary", "arbitrary"), vmem_limit_bytes=VMEM_LIMIT),
        name="conv_mod",
    )(z, z, z, z, z, dw_w, dw_b, ln_g, ln_b, w_pw, b_pw)


def _outproj_kernel(ya_ref, yc_ref, wa_ref, wc_ref, x_ref, gate_ref, fg_ref, o_ref, *, final_norm):
    mixed = jnp.dot(ya_ref[...], wa_ref[...].astype(BF16), preferred_element_type=F32)
    mixed = mixed + jnp.dot(yc_ref[...], wc_ref[...].astype(BF16), preferred_element_type=F32)
    xn = x_ref[...] + gate_ref[0] * mixed
    if final_norm:
        ms = jnp.mean(xn * xn, axis=-1, keepdims=True)
        xn = xn * lax.rsqrt(ms + EPS) * fg_ref[...]
    o_ref[...] = xn


def _out_proj(y_attn, y_conv, w_out, x2, mod3, final_g, *, seq, final_norm, tm=512):
    m, d = x2.shape
    da, dc = y_attn.shape[1], y_conv.shape[1]
    blocks_per_seq = seq // tm
    return pl.pallas_call(
        functools.partial(_outproj_kernel, final_norm=final_norm),
        grid=(m // tm,),
        in_specs=[
            pl.BlockSpec((tm, da), lambda i: (i, 0)),
            pl.BlockSpec((tm, dc), lambda i: (i, 0)),
            pl.BlockSpec((da, d), lambda i: (0, 0)),
            pl.BlockSpec((dc, d), lambda i: (da // dc, 0)),
            pl.BlockSpec((tm, d), lambda i: (i, 0)),
            pl.BlockSpec((1, 1, d), lambda i: ((i // blocks_per_seq) * 3 + 2, 0, 0)),
            pl.BlockSpec((1, d), lambda i: (0, 0)),
        ],
        out_specs=pl.BlockSpec((tm, d), lambda i: (i, 0)),
        out_shape=jax.ShapeDtypeStruct((m, d), F32),
        compiler_params=pltpu.CompilerParams(
            dimension_semantics=("arbitrary",), vmem_limit_bytes=VMEM_LIMIT),
        name="out_proj",
    )(y_attn, y_conv, w_out, w_out, x2, mod3, final_g)


def kernel(x, c, positions, norm_g, w_ada, b_ada, w_in, lambda_q1, lambda_k1, lambda_q2, lambda_k2,
           subln_g, conv_dw_w, conv_dw_b, conv_ln_g, conv_ln_b, w_pw, b_pw, w_out, final_g):
    batch, seq, d = x.shape
    depth = w_in.shape[0]
    d_attn = d // 2
    d_conv = d - d_attn
    dv = d_attn // N_HEADS
    dq = dv // 2
    rot = dq // 4

    tabs = _rope_tables(positions, dq=dq, rot=rot)
    c_pad = jnp.zeros((SUBLANES, d), F32).at[:batch].set(c)

    x2 = x.reshape(batch * seq, d)
    for l in range(depth):
        lam_init = 0.8 - 0.6 * math.exp(-0.3 * l)
        mod = _ada_mod(c_pad, w_ada[l], b_ada[l].reshape(1, -1))
        mod3 = mod.reshape(-1, 1, d)
        z = _in_proj(x2, norm_g[l].reshape(1, d), mod3, tabs, w_in[l],
                     seq=seq, dq=dq, rot=rot, d_qk=2 * d_attn)
        lam_vecs = [v[l].reshape(1, dq) for v in (lambda_q1, lambda_k1, lambda_q2, lambda_k2)]
        y_attn = _diff_attn(z, lam_vecs, subln_g[l].reshape(1, dv),
                            batch=batch, seq=seq, dq=dq, dv=dv, lam_init=lam_init)
        y_conv = _conv_mod(z, conv_dw_w[l], conv_dw_b[l].reshape(1, -1), conv_ln_g[l].reshape(1, -1),
                           conv_ln_b[l].reshape(1, -1), w_pw[l], b_pw[l].reshape(1, -1),
                           batch=batch, seq=seq, c=d_conv, ua_col0=4 * d_attn)
        x2 = _out_proj(y_attn, y_conv, w_out[l], x2, mod3, final_g.reshape(1, d),
                       seq=seq, final_norm=(l == depth - 1))
    return x2.reshape(batch, seq, d)
```

```python
import functools
import math

import jax
import jax.numpy as jnp
from jax import lax
from jax.experimental import pallas as pl
from jax.experimental.pallas import tpu as pltpu

F32 = jnp.float32
BF16 = jnp.bfloat16

N_HEADS = 8
CONV_WIDTH = 31
ROPE_THETA = 500000.0
EPS = 1e-6
LANES = 128
SUBLANES = 8
HALO = 32
NEG_BIG = -1e30
ONES_ROWS = 16
LOG2E = math.log2(math.e)
N_SLOTS = 3
VMEM_LIMIT = 56 * 1024 * 1024


def _aligned(start, multiple):
    return start if isinstance(start, int) else pl.multiple_of(start, multiple)


def _silu(t):
    return t * jax.nn.sigmoid(t)


def _ada_kernel(c_ref, w_ref, b_ref, o_ref):
    ca = _silu(c_ref[...])
    o_ref[...] = jnp.dot(ca.astype(BF16), w_ref[...].astype(BF16),
                         preferred_element_type=F32) + b_ref[...]


def _ada_mod(c_pad, w_ada, b_ada, tn=1024):
    rows, d = c_pad.shape
    n = w_ada.shape[1]
    return pl.pallas_call(
        _ada_kernel,
        grid=(n // tn,),
        in_specs=[pl.BlockSpec((rows, d), lambda j: (0, 0)),
                  pl.BlockSpec((d, tn), lambda j: (0, j)),
                  pl.BlockSpec((1, tn), lambda j: (0, j))],
        out_specs=pl.BlockSpec((rows, tn), lambda j: (0, j)),
        out_shape=jax.ShapeDtypeStruct((rows, n), F32),
        compiler_params=pltpu.CompilerParams(
            dimension_semantics=("arbitrary",), vmem_limit_bytes=VMEM_LIMIT),
        name="ada_mod",
    )(c_pad, w_ada, b_ada)


def _rope_tab_kernel(pos_ref, freq_ref, cos_ref, sin_ref):
    ang = pos_ref[...].astype(F32) * freq_ref[...]
    cos_ref[...] = jnp.cos(ang)
    sin_ref[...] = jnp.sin(ang)


def _rope_tables(positions, *, dq, rot):
    n = positions.size
    nf = rot // 2
    inv_freq = ROPE_THETA ** (-jnp.arange(0, rot, 2, dtype=F32) / rot)
    freq_lane = jnp.tile(inv_freq, LANES // nf).reshape(1, LANES)
    pos_rep = jnp.repeat(positions.reshape(-1), nf).reshape(n * nf // LANES, LANES)
    cos, sin = pl.pallas_call(
        _rope_tab_kernel,
        out_shape=[jax.ShapeDtypeStruct(pos_rep.shape, F32)] * 2,
        name="rope_tab",
    )(pos_rep, freq_lane)
    cos, sin = cos.reshape(n, nf), sin.reshape(n, nf)
    one, zero = jnp.ones((n, dq - rot), F32), jnp.zeros((n, dq - rot), F32)
    tab_c = jnp.tile(jnp.concatenate([cos, cos, one], axis=1), (1, LANES // dq))
    tab_s = jnp.tile(jnp.concatenate([sin, sin, zero], axis=1), (1, LANES // dq))
    return tab_c, tab_s


def _inproj_kernel(xt_ref, xb_ref, g_ref, scale_ref, shift_ref, tc_ref, ts_ref, w_ref, o_ref,
                   h_ref, *, dq, rot, q_scale, n_rope_blocks, chunk, norm_rows_per_iter):
    j = pl.program_id(1)

    @pl.when(j == 0)
    def _():
        gs = g_ref[...] * (1.0 + scale_ref[0])
        sh = shift_ref[0]
        half = xt_ref.shape[0]
        for part, x_ref in enumerate((xt_ref, xb_ref)):

            def norm_rows(r, carry, part=part, x_ref=x_ref):
                r0 = pl.multiple_of(r * norm_rows_per_iter, norm_rows_per_iter)
                x = x_ref[pl.ds(r0, norm_rows_per_iter), :]
                ms = jnp.mean(x * x, axis=-1, keepdims=True)
                h_ref[pl.ds(part * half + r0, norm_rows_per_iter), :] = (
                    x * lax.rsqrt(ms + EPS) * gs + sh).astype(BF16)
                return carry

            lax.fori_loop(0, half // norm_rows_per_iter, norm_rows, 0, unroll=8)

    @pl.when(j < n_rope_blocks)
    def _():
        sc = jnp.where(j < n_rope_blocks // 2, q_scale, 1.0).astype(F32)
        cs, sn = tc_ref[...] * sc, ts_ref[...] * sc
        d = lax.broadcasted_iota(jnp.int32, sn.shape, 1) % dq
        su = jnp.where(d < rot // 2, sn, 0.0)
        sd = jnp.where(d >= rot // 2, sn, 0.0)
        for c in range(w_ref.shape[1] // chunk):
            acc = jnp.dot(h_ref[...], w_ref[:, c * chunk:(c + 1) * chunk].astype(BF16),
                          preferred_element_type=F32)
            for t in range(chunk // LANES):
                a = acc[:, t * LANES:(t + 1) * LANES]
                up = pltpu.roll(a, LANES - rot // 2, 1)
                dn = pltpu.roll(a, rot // 2, 1)
                col = c * chunk + t * LANES
                o_ref[:, col:col + LANES] = (a * cs - up * su + dn * sd).astype(o_ref.dtype)

    @pl.when(j >= n_rope_blocks)
    def _():
        o_ref[...] = jnp.dot(h_ref[...], w_ref[...].astype(BF16),
                             preferred_element_type=F32).astype(o_ref.dtype)


def _in_proj(x2, norm_g, mod3, tabs, w_in, *, seq, dq, rot, d_qk, tm=1024, tn=1024):
    m, d = x2.shape
    n = w_in.shape[1]
    blocks_per_seq = seq // tm
    kern = functools.partial(_inproj_kernel, dq=dq, rot=rot, q_scale=LOG2E / math.sqrt(dq),
                             n_rope_blocks=d_qk // tn, chunk=256, norm_rows_per_iter=16)
    tab_spec = pl.BlockSpec((tm, LANES), lambda i, j: (i, 0))
    n_rows, n_cols = m // tm, n // tn

    def x_half(part):
        switch = n_cols - 2 + part
        return pl.BlockSpec((tm // 2, d), lambda i, j: (
            2 * jnp.where(j >= switch, jnp.minimum(i + 1, n_rows - 1), i) + part, 0))

    return pl.pallas_call(
        kern,
        grid=(n_rows, n_cols),
        in_specs=[
            x_half(0), x_half(1),
            pl.BlockSpec((1, d), lambda i, j: (0, 0)),
            pl.BlockSpec((1, 1, d), lambda i, j: ((i // blocks_per_seq) * 3 + 1, 0, 0)),
            pl.BlockSpec((1, 1, d), lambda i, j: ((i // blocks_per_seq) * 3 + 0, 0, 0)),
            tab_spec, tab_spec,
            pl.BlockSpec((d, tn), lambda i, j: (0, j)),
        ],
        out_specs=pl.BlockSpec((tm, tn), lambda i, j: (i, j)),
        out_shape=jax.ShapeDtypeStruct((m, n), BF16),
        scratch_shapes=[pltpu.VMEM((tm, d), BF16)],
        compiler_params=pltpu.CompilerParams(
            dimension_semantics=("arbitrary", "arbitrary"), vmem_limit_bytes=VMEM_LIMIT),
        name="in_proj",
    )(x2, x2, norm_g, mod3, mod3, *tabs, w_in)


def _attn_kernel(lq1_ref, lk1_ref, lq2_ref, lk2_ref, sg_ref, q_ref, k_ref, v_ref, g_ref, o_ref,
                 vt_ref, qq_ref, s_ref, p_ref, al_ref, m_ref, acc_ref, *, qb, kb, dq, row_chunk, lam_init):
    seq, dv = v_ref.shape
    nq = seq // qb

    vt_ref[0:dv, :] = v_ref[...].astype(F32).T.astype(BF16)
    vt_ref[dv:, :] = jnp.ones((vt_ref.shape[0] - dv, seq), BF16)

    lam = (jnp.exp(jnp.sum(lq1_ref[...] * lk1_ref[...], axis=-1, keepdims=True))
           - jnp.exp(jnp.sum(lq2_ref[...] * lk2_ref[...], axis=-1, keepdims=True))
           + lam_init)

    def build_qq(qi):
        q = q_ref[pl.ds(_aligned(qi * qb, qb), qb), :]
        lane = lax.broadcasted_iota(jnp.int32, q.shape, 1)
        zero = jnp.zeros_like(q)
        qq_ref[0:qb, :] = jnp.where(lane < dq, q, zero)
        qq_ref[qb:, :] = jnp.where(lane >= dq, q, zero)

    nt = (((1,), (1,)), ((), ()))
    half = kb // 2

    def scores(j, slot, diag):
        k0 = _aligned(j * kb, kb)
        if not diag:
            s_ref[slot] = lax.dot_general(k_ref[pl.ds(k0, kb), :], qq_ref[...], nt,
                                          preferred_element_type=F32)
            return
        s_ref[slot, 0:half, :] = lax.dot_general(k_ref[pl.ds(k0, half), :], qq_ref[...], nt,
                                                 preferred_element_type=F32)
        for c in range(2):
            cols = slice(c * qb + half, (c + 1) * qb)
            s_ref[slot, half:kb, cols] = lax.dot_general(k_ref[pl.ds(k0 + half, half), :], qq_ref[cols, :], nt,
                                                         preferred_element_type=F32)

    def softmax(slot, diag):
        for c in range(2):
            cols = slice(c * qb, (c + 1) * qb)

            def chunk(r):
                q0 = half if diag and r >= half else 0
                s = s_ref[slot, r:r + row_chunk, c * qb + q0:(c + 1) * qb]
                if diag:
                    kpos = r + lax.broadcasted_iota(jnp.int32, s.shape, 0)
                    qpos = q0 + lax.broadcasted_iota(jnp.int32, s.shape, 1)
                    s = jnp.where(kpos <= qpos, s, NEG_BIG)
                return s, q0

            part = None
            for r in range(0, kb, row_chunk):
                s, q0 = chunk(r)
                t = jnp.max(s.reshape(row_chunk // SUBLANES, SUBLANES, qb - q0), axis=0)
                if part is None:
                    part = t
                elif q0 == 0:
                    part = jnp.maximum(part, t)
                else:
                    part = jnp.concatenate([part[:, :q0], jnp.maximum(part[:, q0:], t)], axis=1)
            m_old = m_ref[:, cols]
            m_new = jnp.maximum(m_old, jnp.max(part, axis=0, keepdims=True))
            al_ref[slot, :, cols] = jnp.exp2(m_old - m_new)
            m_ref[:, cols] = m_new
            for r in range(0, kb, row_chunk):
                s, q0 = chunk(r)
                p_ref[slot, r:r + row_chunk, c * qb + q0:(c + 1) * qb] = jnp.exp2(s - m_new[:, q0:]).astype(BF16)

    def pv_update(j, slot, diag):
        k0 = _aligned(j * kb, kb)
        if not diag:
            pv = jnp.dot(vt_ref[:, pl.ds(k0, kb)], p_ref[slot], preferred_element_type=F32)
            acc_ref[...] = al_ref[slot] * acc_ref[...] + pv
            return
        pv = jnp.dot(vt_ref[:, pl.ds(k0, half)], p_ref[slot, 0:half, :], preferred_element_type=F32)
        acc_ref[...] = al_ref[slot] * acc_ref[...] + pv
        for c in range(2):
            cols = slice(c * qb + half, (c + 1) * qb)
            acc_ref[:, cols] += jnp.dot(vt_ref[:, pl.ds(k0 + half, half)], p_ref[slot, half:kb, cols],
                                        preferred_element_type=F32)

    def finalize(qi):
        rows = pl.ds(_aligned(qi * qb, qb), qb)
        o_all = acc_ref[0:dv, :] / acc_ref[dv:dv + 1, :]
        o = o_all[:, :qb] - lam * o_all[:, qb:]
        ms = jnp.mean(o * o, axis=0, keepdims=True)
        on = (o * lax.rsqrt(ms + EPS)).T
        on = on * sg_ref[...] * (1.0 - lam_init)
        o_ref[rows, :] = (on * _silu(g_ref[rows, :].astype(F32))).astype(o_ref.dtype)

    pairs = [(qi, j) for qi in range(nq) for j in range(qi + 1)]
    build_qq(0)
    m_ref[...] = jnp.full(m_ref.shape, NEG_BIG, F32)
    acc_ref[...] = jnp.zeros(acc_ref.shape, F32)
    scores(0, 0, diag=True)
    for n, (qi, j) in enumerate(pairs):
        slot, prev, nxt = n % N_SLOTS, (n - 1) % N_SLOTS, (n + 1) % N_SLOTS
        if j == qi and qi + 1 < nq:
            build_qq(qi + 1)
        is_diag = lambda pair: pair[0] == pair[1]
        softmax(slot, diag=is_diag(pairs[n]))
        if n > 0:
            pv_update(pairs[n - 1][1], prev, diag=is_diag(pairs[n - 1]))
        if n + 1 < len(pairs):
            scores(pairs[n + 1][1], nxt, diag=is_diag(pairs[n + 1]))
        if j == 0 and qi > 0:
            finalize(qi - 1)
            acc_ref[...] = jnp.zeros(acc_ref.shape, F32)
        if j == qi:
            m_ref[...] = jnp.full(m_ref.shape, NEG_BIG, F32)
    pv_update(pairs[-1][1], (len(pairs) - 1) % N_SLOTS, diag=True)
    finalize(nq - 1)


def _diff_attn(z, lam_vecs, subln_g, *, batch, seq, dq, dv, lam_init, qb=512):
    kb = qb
    h = N_HEADS
    k_col0 = h * 2 * dq // LANES
    v_col0 = 2 * k_col0
    g_col0 = v_col0 + h * dv // LANES
    kern = functools.partial(_attn_kernel, qb=qb, kb=kb, dq=dq, row_chunk=32, lam_init=lam_init)
    return pl.pallas_call(
        kern,
        grid=(batch, h),
        in_specs=[
            *[pl.BlockSpec((1, dq), lambda b, hh: (0, 0))] * 4,
            pl.BlockSpec((1, dv), lambda b, hh: (0, 0)),
            pl.BlockSpec((seq, 2 * dq), lambda b, hh: (b, hh)),
            pl.BlockSpec((seq, 2 * dq), lambda b, hh: (b, k_col0 + hh)),
            pl.BlockSpec((seq, dv), lambda b, hh: (b, v_col0 + hh)),
            pl.BlockSpec((seq, dv), lambda b, hh: (b, g_col0 + hh)),
        ],
        out_specs=pl.BlockSpec((seq, dv), lambda b, hh: (b, hh)),
        out_shape=jax.ShapeDtypeStruct((batch * seq, h * dv), BF16),
        scratch_shapes=[pltpu.VMEM((dv + ONES_ROWS, seq), BF16),
                        pltpu.VMEM((2 * qb, 2 * dq), BF16),
                        pltpu.VMEM((N_SLOTS, kb, 2 * qb), F32),
                        pltpu.VMEM((N_SLOTS, kb, 2 * qb), BF16),
                        pltpu.VMEM((N_SLOTS, 1, 2 * qb), F32),
                        pltpu.VMEM((1, 2 * qb), F32),
                        pltpu.VMEM((dv + ONES_ROWS, 2 * qb), F32)],
        compiler_params=pltpu.CompilerParams(
            dimension_semantics=("arbitrary", "arbitrary"),
            vmem_limit_bytes=VMEM_LIMIT),
        name="diff_attn",
    )(*lam_vecs, subln_g, z, z, z, z)


def _conv_kernel(ua_ref, ub_ref, ha_ref, hb_ref, gc_ref, dww_ref, dwb_ref, lng_ref, lnb_ref,
                 wpw_ref, bpw_ref, o_ref, sh_ref, wb_ref, conv_ref, *, ts, row_chunk, lane_chunk):
    i = pl.program_id(1)
    c = ua_ref.shape[1]

    @pl.when((pl.program_id(0) == 0) & (i == 0))
    def _():
        for j in range(CONV_WIDTH):
            wb_ref[j] = jnp.broadcast_to(dww_ref[j:j + 1, :], (SUBLANES, c))

    def glu(a_ref, b_ref):
        return a_ref[...].astype(F32) * jax.nn.sigmoid(b_ref[...].astype(F32))

    hist = jnp.where(i > 0, glu(ha_ref, hb_ref), 0.0)
    ypad = jnp.concatenate([hist, glu(ua_ref, ub_ref)], axis=0)
    for r in range(SUBLANES):
        n_r = HALO + ts - (SUBLANES if r else 0)
        sh_ref[r, 0:n_r, 0:c] = ypad[r:r + n_r, :]

    base = HALO - (CONV_WIDTH - 1)
    n_rc = ts // row_chunk
    for lc in range(c // lane_chunk):
        l0 = lc * lane_chunk

        def body(rc, carry, l0=l0):
            r0 = pl.multiple_of(rc * row_chunk, row_chunk)
            parts = []
            for r in range(SUBLANES):
                taps = [j for j in range(CONV_WIDTH) if (base + j) % SUBLANES == r]
                a_lo, a_hi = (base + taps[0]) // SUBLANES, (base + taps[-1]) // SUBLANES
                n_win = row_chunk + (a_hi - a_lo) * SUBLANES
                win = sh_ref[r, pl.ds(r0 + a_lo * SUBLANES, n_win), l0:l0 + lane_chunk]
                win = win.reshape(n_win // SUBLANES, SUBLANES, lane_chunk)
                part = None
                for j in taps:
                    t0 = (base + j) // SUBLANES - a_lo
                    term = win[t0:t0 + row_chunk // SUBLANES] * wb_ref[j, :, l0:l0 + lane_chunk]
                    part = term if part is None else part + term
                parts.append(part)
                if len(parts) == 3:
                    parts = [parts[0] + (parts[1] + parts[2])]
            acc = parts[0] if len(parts) == 1 else parts[0] + parts[1]
            conv_ref[pl.ds(r0, row_chunk), l0:l0 + lane_chunk] = acc.reshape(row_chunk, lane_chunk)
            return carry

        lax.fori_loop(0, n_rc, body, 0)

    y = conv_ref[...] + dwb_ref[...]
    mu = jnp.mean(y, axis=-1, keepdims=True)
    var = jnp.mean(jnp.square(y - mu), axis=-1, keepdims=True)
    y = (y - mu) * lax.rsqrt(var + EPS) * lng_ref[...] + lnb_ref[...]
    y = _silu(y)
    y = jnp.dot(y.astype(BF16), wpw_ref[...].astype(BF16), preferred_element_type=F32) + bpw_ref[...]
    o_ref[...] = (y * _silu(gc_ref[...].astype(F32))).astype(o_ref.dtype)


def _conv_mod(z, dw_w, dw_b, ln_g, ln_b, w_pw, b_pw, *, batch, seq, c, ua_col0, ts=512):
    nt = seq // ts
    ua_blk, ub_blk, gc_blk = ua_col0 // c, ua_col0 // c + 1, ua_col0 // c + 2
    halo_per_ts = ts // HALO
    kern = functools.partial(_conv_kernel, ts=ts, row_chunk=64, lane_chunk=128)

    def halo_map(blk):
        return lambda b, i: (jnp.maximum((b * nt + i) * halo_per_ts - 1, 0), blk)

    row = lambda b, i: (0, 0)
    return pl.pallas_call(
        kern,
        grid=(batch, nt),
        in_specs=[
            pl.BlockSpec((ts, c), lambda b, i: (b * nt + i, ua_blk)),
            pl.BlockSpec((ts, c), lambda b, i: (b * nt + i, ub_blk)),
            pl.BlockSpec((HALO, c), halo_map(ua_blk)),
            pl.BlockSpec((HALO, c), halo_map(ub_blk)),
            pl.BlockSpec((ts, c), lambda b, i: (b * nt + i, gc_blk)),
            pl.BlockSpec((CONV_WIDTH, c), row),
            pl.BlockSpec((1, c), row),
            pl.BlockSpec((1, c), row),
            pl.BlockSpec((1, c), row),
            pl.BlockSpec((c, c), row),
            pl.BlockSpec((1, c), row),
        ],
        out_specs=pl.BlockSpec((ts, c), lambda b, i: (b * nt + i, 0)),
        out_shape=jax.ShapeDtypeStruct((batch * seq, c), BF16),
        scratch_shapes=[pltpu.VMEM((SUBLANES, HALO + ts, c + LANES), F32),
                        pltpu.VMEM((CONV_WIDTH, SUBLANES, c), F32),
                        pltpu.VMEM((ts, c), F32)],
        compiler_params=pltpu.CompilerParams(
            dimension_semantics=("arbitrary", "arbitrary"), vmem_limit_bytes=VMEM_LIMIT),
        name="conv_mod",
    )(z, z, z, z, z, dw_w, dw_b, ln_g, ln_b, w_pw, b_pw)


def _outproj_kernel(ya_ref, yc_ref, wa_ref, wc_ref, x_ref, gate_ref, fg_ref, o_ref, *, final_norm):
    mixed = jnp.dot(ya_ref[...], wa_ref[...].astype(BF16), preferred_element_type=F32)
    mixed = mixed + jnp.dot(yc_ref[...], wc_ref[...].astype(BF16), preferred_element_type=F32)
    xn = x_ref[...] + gate_ref[0] * mixed
    if final_norm:
        ms = jnp.mean(xn * xn, axis=-1, keepdims=True)
        xn = xn * lax.rsqrt(ms + EPS) * fg_ref[...]
    o_ref[...] = xn


def _out_proj(y_attn, y_conv, w_out, x2, mod3, final_g, *, seq, final_norm, tm=512):
    m, d = x2.shape
    da, dc = y_attn.shape[1], y_conv.shape[1]
    blocks_per_seq = seq // tm
    return pl.pallas_call(
        functools.partial(_outproj_kernel, final_norm=final_norm),
        grid=(m // tm,),
        in_specs=[
            pl.BlockSpec((tm, da), lambda i: (i, 0)),
            pl.BlockSpec((tm, dc), lambda i: (i, 0)),
            pl.BlockSpec((da, d), lambda i: (0, 0)),
            pl.BlockSpec((dc, d), lambda i: (da // dc, 0)),
            pl.BlockSpec((tm, d), lambda i: (i, 0)),
            pl.BlockSpec((1, 1, d), lambda i: ((i // blocks_per_seq) * 3 + 2, 0, 0)),
            pl.BlockSpec((1, d), lambda i: (0, 0)),
        ],
        out_specs=pl.BlockSpec((tm, d), lambda i: (i, 0)),
        out_shape=jax.ShapeDtypeStruct((m, d), F32),
        compiler_params=pltpu.CompilerParams(
            dimension_semantics=("arbitrary",), vmem_limit_bytes=VMEM_LIMIT),
        name="out_proj",
    )(y_attn, y_conv, w_out, w_out, x2, mod3, final_g)


def kernel(x, c, positions, norm_g, w_ada, b_ada, w_in, lambda_q1, lambda_k1, lambda_q2, lambda_k2,
           subln_g, conv_dw_w, conv_dw_b, conv_ln_g, conv_ln_b, w_pw, b_pw, w_out, final_g):
    batch, seq, d = x.shape
    depth = w_in.shape[0]
    d_attn = d // 2
    d_conv = d - d_attn
    dv = d_attn // N_HEADS
    dq = dv // 2
    rot = dq // 4

    tabs = _rope_tables(positions, dq=dq, rot=rot)
    c_pad = jnp.zeros((SUBLANES, d), F32).at[:batch].set(c)

    x2 = x.reshape(batch * seq, d)
    for l in range(depth):
        lam_init = 0.8 - 0.6 * math.exp(-0.3 * l)
        mod = _ada_mod(c_pad, w_ada[l], b_ada[l].reshape(1, -1))
        mod3 = mod.reshape(-1, 1, d)
        z = _in_proj(x2, norm_g[l].reshape(1, d), mod3, tabs, w_in[l],
                     seq=seq, dq=dq, rot=rot, d_qk=2 * d_attn)
        lam_vecs = [v[l].reshape(1, dq) for v in (lambda_q1, lambda_k1, lambda_q2, lambda_k2)]
        y_attn = _diff_attn(z, lam_vecs, subln_g[l].reshape(1, dv),
                            batch=batch, seq=seq, dq=dq, dv=dv, lam_init=lam_init)
        y_conv = _conv_mod(z, conv_dw_w[l], conv_dw_b[l].reshape(1, -1), conv_ln_g[l].reshape(1, -1),
                           conv_ln_b[l].reshape(1, -1), w_pw[l], b_pw[l].reshape(1, -1),
                           batch=batch, seq=seq, c=d_conv, ua_col0=4 * d_attn)
        x2 = _out_proj(y_attn, y_conv, w_out[l], x2, mod3, final_g.reshape(1, d),
                       seq=seq, final_norm=(l == depth - 1))
    return x2.reshape(batch, seq, d)
```

```python
import functools
import math

import jax
import jax.numpy as jnp
from jax import lax
from jax.experimental import pallas as pl
from jax.experimental.pallas import tpu as pltpu

F32 = jnp.float32
BF16 = jnp.bfloat16

N_HEADS = 8
CONV_WIDTH = 31
ROPE_THETA = 500000.0
EPS = 1e-6
LANES = 128
SUBLANES = 8
HALO = 32
NEG_BIG = -1e30
ONES_ROWS = 16
LOG2E = math.log2(math.e)
N_SLOTS = 3
VMEM_LIMIT = 56 * 1024 * 1024


def _aligned(start, multiple):
    return start if isinstance(start, int) else pl.multiple_of(start, multiple)


def _silu(t):
    return t * jax.nn.sigmoid(t)


def _ada_kernel(c_ref, w_ref, b_ref, o_ref):
    ca = _silu(c_ref[...])
    o_ref[...] = jnp.dot(ca.astype(BF16), w_ref[...].astype(BF16),
                         preferred_element_type=F32) + b_ref[...]


def _ada_mod(c_pad, w_ada, b_ada, tn=1024):
    rows, d = c_pad.shape
    n = w_ada.shape[1]
    return pl.pallas_call(
        _ada_kernel,
        grid=(n // tn,),
        in_specs=[pl.BlockSpec((rows, d), lambda j: (0, 0)),
                  pl.BlockSpec((d, tn), lambda j: (0, j)),
                  pl.BlockSpec((1, tn), lambda j: (0, j))],
        out_specs=pl.BlockSpec((rows, tn), lambda j: (0, j)),
        out_shape=jax.ShapeDtypeStruct((rows, n), F32),
        compiler_params=pltpu.CompilerParams(
            dimension_semantics=("arbitrary",), vmem_limit_bytes=VMEM_LIMIT),
        name="ada_mod",
    )(c_pad, w_ada, b_ada)


def _rope_tab_kernel(pos_ref, freq_ref, cos_ref, sin_ref):
    ang = pos_ref[...].astype(F32) * freq_ref[...]
    cos_ref[...] = jnp.cos(ang)
    sin_ref[...] = jnp.sin(ang)


def _rope_tables(positions, *, dq, rot):
    n = positions.size
    nf = rot // 2
    inv_freq = ROPE_THETA ** (-jnp.arange(0, rot, 2, dtype=F32) / rot)
    freq_lane = jnp.tile(inv_freq, LANES // nf).reshape(1, LANES)
    pos_rep = jnp.repeat(positions.reshape(-1), nf).reshape(n * nf // LANES, LANES)
    cos, sin = pl.pallas_call(
        _rope_tab_kernel,
        out_shape=[jax.ShapeDtypeStruct(pos_rep.shape, F32)] * 2,
        name="rope_tab",
    )(pos_rep, freq_lane)
    cos, sin = cos.reshape(n, nf), sin.reshape(n, nf)
    one, zero = jnp.ones((n, dq - rot), F32), jnp.zeros((n, dq - rot), F32)
    tab_c = jnp.tile(jnp.concatenate([cos, cos, one], axis=1), (1, LANES // dq))
    tab_s = jnp.tile(jnp.concatenate([sin, sin, zero], axis=1), (1, LANES // dq))
    return tab_c, tab_s


def _inproj_kernel(xt_ref, xb_ref, g_ref, scale_ref, shift_ref, tc_ref, ts_ref, w_ref, o_ref,
                   h_ref, *, dq, rot, q_scale, n_rope_blocks, chunk, norm_rows_per_iter):
    j = pl.program_id(1)

    @pl.when(j == 0)
    def _():
        gs = g_ref[...] * (1.0 + scale_ref[0])
        sh = shift_ref[0]
        half = xt_ref.shape[0]
        for part, x_ref in enumerate((xt_ref, xb_ref)):

            def norm_rows(r, carry, part=part, x_ref=x_ref):
                r0 = pl.multiple_of(r * norm_rows_per_iter, norm_rows_per_iter)
                x = x_ref[pl.ds(r0, norm_rows_per_iter), :]
                ms = jnp.mean(x * x, axis=-1, keepdims=True)
                h_ref[pl.ds(part * half + r0, norm_rows_per_iter), :] = (
                    x * lax.rsqrt(ms + EPS) * gs + sh).astype(BF16)
                return carry

            lax.fori_loop(0, half // norm_rows_per_iter, norm_rows, 0, unroll=8)

    @pl.when(j < n_rope_blocks)
    def _():
        sc = jnp.where(j < n_rope_blocks // 2, q_scale, 1.0).astype(F32)
        cs, sn = tc_ref[...] * sc, ts_ref[...] * sc
        d = lax.broadcasted_iota(jnp.int32, sn.shape, 1) % dq
        su = jnp.where(d < rot // 2, sn, 0.0)
        sd = jnp.where(d >= rot // 2, sn, 0.0)
        for c in range(w_ref.shape[1] // chunk):
            acc = jnp.dot(h_ref[...], w_ref[:, c * chunk:(c + 1) * chunk].astype(BF16),
                          preferred_element_type=F32)
            for t in range(chunk // LANES):
                a = acc[:, t * LANES:(t + 1) * LANES]
                up = pltpu.roll(a, LANES - rot // 2, 1)
                dn = pltpu.roll(a, rot // 2, 1)
                col = c * chunk + t * LANES
                o_ref[:, col:col + LANES] = (a * cs - up * su + dn * sd).astype(o_ref.dtype)

    @pl.when(j >= n_rope_blocks)
    def _():
        o_ref[...] = jnp.dot(h_ref[...], w_ref[...].astype(BF16),
                             preferred_element_type=F32).astype(o_ref.dtype)


def _in_proj(x2, norm_g, mod3, tabs, w_in, *, seq, dq, rot, d_qk, tm=1024, tn=1024):
    m, d = x2.shape
    n = w_in.shape[1]
    blocks_per_seq = seq // tm
    kern = functools.partial(_inproj_kernel, dq=dq, rot=rot, q_scale=LOG2E / math.sqrt(dq),
                             n_rope_blocks=d_qk // tn, chunk=256, norm_rows_per_iter=16)
    tab_spec = pl.BlockSpec((tm, LANES), lambda i, j: (i, 0))
    n_rows, n_cols = m // tm, n // tn

    def x_half(part):
        switch = n_cols - 2 + part
        return pl.BlockSpec((tm // 2, d), lambda i, j: (
            2 * jnp.where(j >= switch, jnp.minimum(i + 1, n_rows - 1), i) + part, 0))

    return pl.pallas_call(
        kern,
        grid=(n_rows, n_cols),
        in_specs=[
            x_half(0), x_half(1),
            pl.BlockSpec((1, d), lambda i, j: (0, 0)),
            pl.BlockSpec((1, 1, d), lambda i, j: ((i // blocks_per_seq) * 3 + 1, 0, 0)),
            pl.BlockSpec((1, 1, d), lambda i, j: ((i // blocks_per_seq) * 3 + 0, 0, 0)),
            tab_spec, tab_spec,
            pl.BlockSpec((d, tn), lambda i, j: (0, j)),
        ],
        out_specs=pl.BlockSpec((tm, tn), lambda i, j: (i, j)),
        out_shape=jax.ShapeDtypeStruct((m, n), BF16),
        scratch_shapes=[pltpu.VMEM((tm, d), BF16)],
        compiler_params=pltpu.CompilerParams(
            dimension_semantics=("arbitrary", "arbitrary"), vmem_limit_bytes=VMEM_LIMIT),
        name="in_proj",
    )(x2, x2, norm_g, mod3, mod3, *tabs, w_in)


def _attn_kernel(lq1_ref, lk1_ref, lq2_ref, lk2_ref, sg_ref, q_ref, k_ref, v_ref, g_ref, o_ref,
                 vt_ref, qq_ref, s_ref, p_ref, al_ref, m_ref, acc_ref, *, qb, kb, dq, row_chunk, lam_init):
    seq, dv = v_ref.shape
    nq = seq // qb

    vt_ref[0:dv, :] = v_ref[...].astype(F32).T.astype(BF16)
    vt_ref[dv:, :] = jnp.ones((vt_ref.shape[0] - dv, seq), BF16)

    lam = (jnp.exp(jnp.sum(lq1_ref[...] * lk1_ref[...], axis=-1, keepdims=True))
           - jnp.exp(jnp.sum(lq2_ref[...] * lk2_ref[...], axis=-1, keepdims=True))
           + lam_init)

    def build_qq(qi):
        q = q_ref[pl.ds(_aligned(qi * qb, qb), qb), :]
        lane = lax.broadcasted_iota(jnp.int32, q.shape, 1)
        zero = jnp.zeros_like(q)
        qq_ref[0:qb, :] = jnp.where(lane < dq, q, zero)
        qq_ref[qb:, :] = jnp.where(lane >= dq, q, zero)

    nt = (((1,), (1,)), ((), ()))
    half = kb // 2

    def scores(j, slot, diag):
        k0 = _aligned(j * kb, kb)
        if not diag:
            s_ref[slot] = lax.dot_general(k_ref[pl.ds(k0, kb), :], qq_ref[...], nt,
                                          preferred_element_type=F32)
            return
        s_ref[slot, 0:half, :] = lax.dot_general(k_ref[pl.ds(k0, half), :], qq_ref[...], nt,
                                                 preferred_element_type=F32)
        for c in range(2):
            cols = slice(c * qb + half, (c + 1) * qb)
            s_ref[slot, half:kb, cols] = lax.dot_general(k_ref[pl.ds(k0 + half, half), :], qq_ref[cols, :], nt,
                                                         preferred_element_type=F32)

    def softmax(slot, diag):
        for c in range(2):
            cols = slice(c * qb, (c + 1) * qb)

            def chunk(r):
                q0 = half if diag and r >= half else 0
                s = s_ref[slot, r:r + row_chunk, c * qb + q0:(c + 1) * qb]
                if diag:
                    kpos = r + lax.broadcasted_iota(jnp.int32, s.shape, 0)
                    qpos = q0 + lax.broadcasted_iota(jnp.int32, s.shape, 1)
                    s = jnp.where(kpos <= qpos, s, NEG_BIG)
                return s, q0

            part = None
            for r in range(0, kb, row_chunk):
                s, q0 = chunk(r)
                t = jnp.max(s.reshape(row_chunk // SUBLANES, SUBLANES, qb - q0), axis=0)
                if part is None:
                    part = t
                elif q0 == 0:
                    part = jnp.maximum(part, t)
                else:
                    part = jnp.concatenate([part[:, :q0], jnp.maximum(part[:, q0:], t)], axis=1)
            m_old = m_ref[:, cols]
            m_new = jnp.maximum(m_old, jnp.max(part, axis=0, keepdims=True))
            al_ref[slot, :, cols] = jnp.exp2(m_old - m_new)
            m_ref[:, cols] = m_new
            for r in range(0, kb, row_chunk):
                s, q0 = chunk(r)
                p_ref[slot, r:r + row_chunk, c * qb + q0:(c + 1) * qb] = jnp.exp2(s - m_new[:, q0:]).astype(BF16)

    def pv_update(j, slot, diag):
        k0 = _aligned(j * kb, kb)
        nk = half if diag else kb
        pv = jnp.dot(vt_ref[:, pl.ds(k0, nk)], p_ref[slot, 0:nk, :], preferred_element_type=F32)
        acc_ref[...] = pv if j == 0 else al_ref[slot] * acc_ref[...] + pv
        if not diag:
            return
        for c in range(2):
            cols = slice(c * qb + half, (c + 1) * qb)
            acc_ref[:, cols] += jnp.dot(vt_ref[:, pl.ds(k0 + half, half)], p_ref[slot, half:kb, cols],
                                        preferred_element_type=F32)

    def finalize(qi):
        rows = pl.ds(_aligned(qi * qb, qb), qb)
        o_all = acc_ref[0:dv, :] * (1.0 / acc_ref[dv:dv + 1, :])
        o = o_all[:, :qb] - lam * o_all[:, qb:]
        ms = jnp.mean(o * o, axis=0, keepdims=True)
        on = (o * lax.rsqrt(ms + EPS)).T
        on = on * sg_ref[...] * (1.0 - lam_init)
        o_ref[rows, :] = (on * _silu(g_ref[rows, :].astype(F32))).astype(o_ref.dtype)

    pairs = [(qi, j) for qi in range(nq) for j in range(qi + 1)]
    build_qq(0)
    m_ref[...] = jnp.full(m_ref.shape, NEG_BIG, F32)
    scores(0, 0, diag=True)
    for n, (qi, j) in enumerate(pairs):
        slot, prev, nxt = n % N_SLOTS, (n - 1) % N_SLOTS, (n + 1) % N_SLOTS
        if j == qi and qi + 1 < nq:
            build_qq(qi + 1)
        is_diag = lambda pair: pair[0] == pair[1]
        softmax(slot, diag=is_diag(pairs[n]))
        if n > 0:
            pv_update(pairs[n - 1][1], prev, diag=is_diag(pairs[n - 1]))
        if n + 1 < len(pairs):
            scores(pairs[n + 1][1], nxt, diag=is_diag(pairs[n + 1]))
        if j == 0 and qi > 0:
            finalize(qi - 1)
        if j == qi:
            m_ref[...] = jnp.full(m_ref.shape, NEG_BIG, F32)
    pv_update(pairs[-1][1], (len(pairs) - 1) % N_SLOTS, diag=True)
    finalize(nq - 1)


def _diff_attn(z, lam_vecs, subln_g, *, batch, seq, dq, dv, lam_init, qb=512):
    kb = qb
    h = N_HEADS
    k_col0 = h * 2 * dq // LANES
    v_col0 = 2 * k_col0
    g_col0 = v_col0 + h * dv // LANES
    kern = functools.partial(_attn_kernel, qb=qb, kb=kb, dq=dq, row_chunk=32, lam_init=lam_init)
    return pl.pallas_call(
        kern,
        grid=(batch, h),
        in_specs=[
            *[pl.BlockSpec((1, dq), lambda b, hh: (0, 0))] * 4,
            pl.BlockSpec((1, dv), lambda b, hh: (0, 0)),
            pl.BlockSpec((seq, 2 * dq), lambda b, hh: (b, hh)),
            pl.BlockSpec((seq, 2 * dq), lambda b, hh: (b, k_col0 + hh)),
            pl.BlockSpec((seq, dv), lambda b, hh: (b, v_col0 + hh)),
            pl.BlockSpec((seq, dv), lambda b, hh: (b, g_col0 + hh)),
        ],
        out_specs=pl.BlockSpec((seq, dv), lambda b, hh: (b, hh)),
        out_shape=jax.ShapeDtypeStruct((batch * seq, h * dv), BF16),
        scratch_shapes=[pltpu.VMEM((dv + ONES_ROWS, seq), BF16),
                        pltpu.VMEM((2 * qb, 2 * dq), BF16),
                        pltpu.VMEM((N_SLOTS, kb, 2 * qb), F32),
                        pltpu.VMEM((N_SLOTS, kb, 2 * qb), BF16),
                        pltpu.VMEM((N_SLOTS, 1, 2 * qb), F32),
                        pltpu.VMEM((1, 2 * qb), F32),
                        pltpu.VMEM((dv + ONES_ROWS, 2 * qb), F32)],
        compiler_params=pltpu.CompilerParams(
            dimension_semantics=("arbitrary", "arbitrary"),
            vmem_limit_bytes=VMEM_LIMIT),
        name="diff_attn",
    )(*lam_vecs, subln_g, z, z, z, z)


def _conv_kernel(ua_ref, ub_ref, ha_ref, hb_ref, gc_ref, dww_ref, dwb_ref, lng_ref, lnb_ref,
                 wpw_ref, bpw_ref, o_ref, sh_ref, wb_ref, conv_ref, *, ts, row_chunk, lane_chunk):
    i = pl.program_id(1)
    c = ua_ref.shape[1]

    @pl.when((pl.program_id(0) == 0) & (i == 0))
    def _():
        for j in range(CONV_WIDTH):
            wb_ref[j] = jnp.broadcast_to(dww_ref[j:j + 1, :], (SUBLANES, c))

    def glu(a_ref, b_ref):
        return a_ref[...].astype(F32) * jax.nn.sigmoid(b_ref[...].astype(F32))

    hist = jnp.where(i > 0, glu(ha_ref, hb_ref), 0.0)
    ypad = jnp.concatenate([hist, glu(ua_ref, ub_ref)], axis=0)
    for r in range(SUBLANES):
        n_r = HALO + ts - (SUBLANES if r else 0)
        sh_ref[r, 0:n_r, 0:c] = ypad[r:r + n_r, :]

    base = HALO - (CONV_WIDTH - 1)
    n_rc = ts // row_chunk
    for lc in range(c // lane_chunk):
        l0 = lc * lane_chunk

        def body(rc, carry, l0=l0):
            r0 = pl.multiple_of(rc * row_chunk, row_chunk)
            parts = []
            for r in range(SUBLANES):
                taps = [j for j in range(CONV_WIDTH) if (base + j) % SUBLANES == r]
                a_lo, a_hi = (base + taps[0]) // SUBLANES, (base + taps[-1]) // SUBLANES
                n_win = row_chunk + (a_hi - a_lo) * SUBLANES
                win = sh_ref[r, pl.ds(r0 + a_lo * SUBLANES, n_win), l0:l0 + lane_chunk]
                win = win.reshape(n_win // SUBLANES, SUBLANES, lane_chunk)
                part = None
                for j in taps:
                    t0 = (base + j) // SUBLANES - a_lo
                    term = win[t0:t0 + row_chunk // SUBLANES] * wb_ref[j, :, l0:l0 + lane_chunk]
                    part = term if part is None else part + term
                parts.append(part)
                if len(parts) == 3:
                    parts = [parts[0] + (parts[1] + parts[2])]
            acc = parts[0] if len(parts) == 1 else parts[0] + parts[1]
            conv_ref[pl.ds(r0, row_chunk), l0:l0 + lane_chunk] = acc.reshape(row_chunk, lane_chunk)
            return carry

        lax.fori_loop(0, n_rc, body, 0)

    y = conv_ref[...] + dwb_ref[...]
    mu = jnp.mean(y, axis=-1, keepdims=True)
    var = jnp.mean(jnp.square(y - mu), axis=-1, keepdims=True)
    y = (y - mu) * lax.rsqrt(var + EPS) * lng_ref[...] + lnb_ref[...]
    y = _silu(y)
    y = jnp.dot(y.astype(BF16), wpw_ref[...].astype(BF16), preferred_element_type=F32) + bpw_ref[...]
    o_ref[...] = (y * _silu(gc_ref[...].astype(F32))).astype(o_ref.dtype)


def _conv_mod(z, dw_w, dw_b, ln_g, ln_b, w_pw, b_pw, *, batch, seq, c, ua_col0, ts=512):
    nt = seq // ts
    ua_blk, ub_blk, gc_blk = ua_col0 // c, ua_col0 // c + 1, ua_col0 // c + 2
    halo_per_ts = ts // HALO
    kern = functools.partial(_conv_kernel, ts=ts, row_chunk=64, lane_chunk=128)

    def halo_map(blk):
        return lambda b, i: (jnp.maximum((b * nt + i) * halo_per_ts - 1, 0), blk)

    row = lambda b, i: (0, 0)
    return pl.pallas_call(
        kern,
        grid=(batch, nt),
        in_specs=[
            pl.BlockSpec((ts, c), lambda b, i: (b * nt + i, ua_blk)),
            pl.BlockSpec((ts, c), lambda b, i: (b * nt + i, ub_blk)),
            pl.BlockSpec((HALO, c), halo_map(ua_blk)),
            pl.BlockSpec((HALO, c), halo_map(ub_blk)),
            pl.BlockSpec((ts, c), lambda b, i: (b * nt + i, gc_blk)),
            pl.BlockSpec((CONV_WIDTH, c), row),
            pl.BlockSpec((1, c), row),
            pl.BlockSpec((1, c), row),
            pl.BlockSpec((1, c), row),
            pl.BlockSpec((c, c), row),
            pl.BlockSpec((1, c), row),
        ],
        out_specs=pl.BlockSpec((ts, c), lambda b, i: (b * nt + i, 0)),
        out_shape=jax.ShapeDtypeStruct((batch * seq, c), BF16),
        scratch_shapes=[pltpu.VMEM((SUBLANES, HALO + ts, c + LANES), F32),
                        pltpu.VMEM((CONV_WIDTH, SUBLANES, c), F32),
                        pltpu.VMEM((ts, c), F32)],
        compiler_params=pltpu.CompilerParams(
            dimension_semantics=("arbitrary", "arbitrary"), vmem_limit_bytes=VMEM_LIMIT),
        name="conv_mod",
    )(z, z, z, z, z, dw_w, dw_b, ln_g, ln_b, w_pw, b_pw)


def _outproj_kernel(ya_ref, yc_ref, wa_ref, wc_ref, x_ref, gate_ref, fg_ref, o_ref, *, final_norm):
    mixed = jnp.dot(ya_ref[...], wa_ref[...].astype(BF16), preferred_element_type=F32)
    mixed = mixed + jnp.dot(yc_ref[...], wc_ref[...].astype(BF16), preferred_element_type=F32)
    xn = x_ref[...] + gate_ref[0] * mixed
    if final_norm:
        ms = jnp.mean(xn * xn, axis=-1, keepdims=True)
        xn = xn * lax.rsqrt(ms + EPS) * fg_ref[...]
    o_ref[...] = xn


def _out_proj(y_attn, y_conv, w_out, x2, mod3, final_g, *, seq, final_norm, tm=512):
    m, d = x2.shape
    da, dc = y_attn.shape[1], y_conv.shape[1]
    blocks_per_seq = seq // tm
    return pl.pallas_call(
        functools.partial(_outproj_kernel, final_norm=final_norm),
        grid=(m // tm,),
        in_specs=[
            pl.BlockSpec((tm, da), lambda i: (i, 0)),
            pl.BlockSpec((tm, dc), lambda i: (i, 0)),
            pl.BlockSpec((da, d), lambda i: (0, 0)),
            pl.BlockSpec((dc, d), lambda i: (da // dc, 0)),
            pl.BlockSpec((tm, d), lambda i: (i, 0)),
            pl.BlockSpec((1, 1, d), lambda i: ((i // blocks_per_seq) * 3 + 2, 0, 0)),
            pl.BlockSpec((1, d), lambda i: (0, 0)),
        ],
        out_specs=pl.BlockSpec((tm, d), lambda i: (i, 0)),
        out_shape=jax.ShapeDtypeStruct((m, d), F32),
        compiler_params=pltpu.CompilerParams(
            dimension_semantics=("arbitrary",), vmem_limit_bytes=VMEM_LIMIT),
        name="out_proj",
    )(y_attn, y_conv, w_out, w_out, x2, mod3, final_g)


def kernel(x, c, positions, norm_g, w_ada, b_ada, w_in, lambda_q1, lambda_k1, lambda_q2, lambda_k2,
           subln_g, conv_dw_w, conv_dw_b, conv_ln_g, conv_ln_b, w_pw, b_pw, w_out, final_g):
    batch, seq, d = x.shape
    depth = w_in.shape[0]
    d_attn = d // 2
    d_conv = d - d_attn
    dv = d_attn // N_HEADS
    dq = dv // 2
    rot = dq // 4

    tabs = _rope_tables(positions, dq=dq, rot=rot)
    c_pad = jnp.zeros((SUBLANES, d), F32).at[:batch].set(c)

    x2 = x.reshape(batch * seq, d)
    for l in range(depth):
        lam_init = 0.8 - 0.6 * math.exp(-0.3 * l)
        mod = _ada_mod(c_pad, w_ada[l], b_ada[l].reshape(1, -1))
        mod3 = mod.reshape(-1, 1, d)
        z = _in_proj(x2, norm_g[l].reshape(1, d), mod3, tabs, w_in[l],
                     seq=seq, dq=dq, rot=rot, d_qk=2 * d_attn)
        lam_vecs = [v[l].reshape(1, dq) for v in (lambda_q1, lambda_k1, lambda_q2, lambda_k2)]
        y_attn = _diff_attn(z, lam_vecs, subln_g[l].reshape(1, dv),
                            batch=batch, seq=seq, dq=dq, dv=dv, lam_init=lam_init)
        y_conv = _conv_mod(z, conv_dw_w[l], conv_dw_b[l].reshape(1, -1), conv_ln_g[l].reshape(1, -1),
                           conv_ln_b[l].reshape(1, -1), w_pw[l], b_pw[l].reshape(1, -1),
                           batch=batch, seq=seq, c=d_conv, ua_col0=4 * d_attn)
        x2 = _out_proj(y_attn, y_conv, w_out[l], x2, mod3, final_g.reshape(1, d),
                       seq=seq, final_norm=(l == depth - 1))
    return x2.reshape(batch, seq, d)
```

```python
import functools
import math

import jax
import jax.numpy as jnp
from jax import lax
from jax.experimental import pallas as pl
from jax.experimental.pallas import tpu as pltpu

F32 = jnp.float32
BF16 = jnp.bfloat16

N_HEADS = 8
CONV_WIDTH = 31
ROPE_THETA = 500000.0
EPS = 1e-6
LANES = 128
SUBLANES = 8
HALO = 32
NEG_BIG = -1e30
ONES_ROWS = 16
LOG2E = math.log2(math.e)
N_SLOTS = 3
VMEM_LIMIT = 56 * 1024 * 1024


def _aligned(start, multiple):
    return start if isinstance(start, int) else pl.multiple_of(start, multiple)


def _silu(t):
    return t * jax.nn.sigmoid(t)


def _ada_kernel(c_ref, w_hbm, b_ref, o_ref, buf_ref, sem_ref, *, tn, n_buf):
    n_chunks = o_ref.shape[1] // tn

    def chunk_copy(k):
        return pltpu.make_async_copy(w_hbm.at[:, pl.ds(k * tn, tn)], buf_ref.at[k % n_buf], sem_ref.at[k % n_buf])

    for k in range(min(n_buf, n_chunks)):
        chunk_copy(k).start()
    ca = _silu(c_ref[...]).astype(BF16)
    for k in range(n_chunks):
        cols = slice(k * tn, (k + 1) * tn)
        chunk_copy(k).wait()
        o_ref[:, cols] = jnp.dot(ca, buf_ref[k % n_buf].astype(BF16), preferred_element_type=F32) + b_ref[:, cols]
        if k + n_buf < n_chunks:
            chunk_copy(k + n_buf).start()


def _ada_mod(c_pad, w_ada, b_ada, tn=768, n_buf=3):
    rows, d = c_pad.shape
    n = w_ada.shape[1]
    return pl.pallas_call(
        functools.partial(_ada_kernel, tn=tn, n_buf=n_buf),
        in_specs=[pl.BlockSpec(memory_space=pltpu.VMEM),
                  pl.BlockSpec(memory_space=pl.ANY),
                  pl.BlockSpec(memory_space=pltpu.VMEM)],
        out_specs=pl.BlockSpec(memory_space=pltpu.VMEM),
        out_shape=jax.ShapeDtypeStruct((rows, n), F32),
        scratch_shapes=[pltpu.VMEM((n_buf, d, tn), F32),
                        pltpu.SemaphoreType.DMA((n_buf,))],
        compiler_params=pltpu.CompilerParams(vmem_limit_bytes=VMEM_LIMIT),
        name="ada_mod",
    )(c_pad, w_ada, b_ada)


def _rope_tab_kernel(pos_ref, freq_ref, cos_ref, sin_ref):
    ang = pos_ref[...].astype(F32) * freq_ref[...]
    cos_ref[...] = jnp.cos(ang)
    sin_ref[...] = jnp.sin(ang)


def _rope_tables(positions, *, dq, rot):
    n = positions.size
    nf = rot // 2
    inv_freq = ROPE_THETA ** (-jnp.arange(0, rot, 2, dtype=F32) / rot)
    freq_lane = jnp.tile(inv_freq, LANES // nf).reshape(1, LANES)
    pos_rep = jnp.repeat(positions.reshape(-1), nf).reshape(n * nf // LANES, LANES)
    cos, sin = pl.pallas_call(
        _rope_tab_kernel,
        out_shape=[jax.ShapeDtypeStruct(pos_rep.shape, F32)] * 2,
        name="rope_tab",
    )(pos_rep, freq_lane)
    cos, sin = cos.reshape(n, nf), sin.reshape(n, nf)
    one, zero = jnp.ones((n, dq - rot), F32), jnp.zeros((n, dq - rot), F32)
    tab_c = jnp.tile(jnp.concatenate([cos, cos, one], axis=1), (1, LANES // dq))
    tab_s = jnp.tile(jnp.concatenate([sin, sin, zero], axis=1), (1, LANES // dq))
    return tab_c, tab_s


def _inproj_kernel(xt_ref, xb_ref, g_ref, scale_ref, shift_ref, tc_ref, ts_ref, w_ref, o_ref,
                   h_ref, *, dq, rot, q_scale, n_rope_blocks, chunk, norm_rows_per_iter):
    j = pl.program_id(1)

    @pl.when(j == 0)
    def _():
        gs = g_ref[...] * (1.0 + scale_ref[0])
        sh = shift_ref[0]
        half = xt_ref.shape[0]
        for part, x_ref in enumerate((xt_ref, xb_ref)):

            def norm_rows(r, carry, part=part, x_ref=x_ref):
                r0 = pl.multiple_of(r * norm_rows_per_iter, norm_rows_per_iter)
                x = x_ref[pl.ds(r0, norm_rows_per_iter), :]
                ms = jnp.mean(x * x, axis=-1, keepdims=True)
                h_ref[pl.ds(part * half + r0, norm_rows_per_iter), :] = (
                    x * lax.rsqrt(ms + EPS) * gs + sh).astype(BF16)
                return carry

            lax.fori_loop(0, half // norm_rows_per_iter, norm_rows, 0, unroll=8)

    @pl.when(j < n_rope_blocks)
    def _():
        sc = jnp.where(j < n_rope_blocks // 2, q_scale, 1.0).astype(F32)
        cs, sn = tc_ref[...] * sc, ts_ref[...] * sc
        d = lax.broadcasted_iota(jnp.int32, sn.shape, 1) % dq
        su = jnp.where(d < rot // 2, sn, 0.0)
        sd = jnp.where(d >= rot // 2, sn, 0.0)
        for c in range(w_ref.shape[1] // chunk):
            acc = jnp.dot(h_ref[...], w_ref[:, c * chunk:(c + 1) * chunk].astype(BF16),
                          preferred_element_type=F32)
            for t in range(chunk // LANES):
                a = acc[:, t * LANES:(t + 1) * LANES]
                up = pltpu.roll(a, LANES - rot // 2, 1)
                dn = pltpu.roll(a, rot // 2, 1)
                col = c * chunk + t * LANES
                o_ref[:, col:col + LANES] = (a * cs - up * su + dn * sd).astype(o_ref.dtype)

    @pl.when(j >= n_rope_blocks)
    def _():
        o_ref[...] = jnp.dot(h_ref[...], w_ref[...].astype(BF16),
                             preferred_element_type=F32).astype(o_ref.dtype)


def _in_proj(x2, norm_g, mod3, tabs, w_in, *, seq, dq, rot, d_qk, tm=1024, tn=1024):
    m, d = x2.shape
    n = w_in.shape[1]
    blocks_per_seq = seq // tm
    kern = functools.partial(_inproj_kernel, dq=dq, rot=rot, q_scale=LOG2E / math.sqrt(dq),
                             n_rope_blocks=d_qk // tn, chunk=256, norm_rows_per_iter=16)
    tab_spec = pl.BlockSpec((tm, LANES), lambda i, j: (i, 0))
    n_rows, n_cols = m // tm, n // tn

    def x_half(part):
        switch = n_cols - 2 + part
        return pl.BlockSpec((tm // 2, d), lambda i, j: (
            2 * jnp.where(j >= switch, jnp.minimum(i + 1, n_rows - 1), i) + part, 0))

    return pl.pallas_call(
        kern,
        grid=(n_rows, n_cols),
        in_specs=[
            x_half(0), x_half(1),
            pl.BlockSpec((1, d), lambda i, j: (0, 0)),
            pl.BlockSpec((1, 1, d), lambda i, j: ((i // blocks_per_seq) * 3 + 1, 0, 0)),
            pl.BlockSpec((1, 1, d), lambda i, j: ((i // blocks_per_seq) * 3 + 0, 0, 0)),
            tab_spec, tab_spec,
            pl.BlockSpec((d, tn), lambda i, j: (0, j)),
        ],
        out_specs=pl.BlockSpec((tm, tn), lambda i, j: (i, j)),
        out_shape=jax.ShapeDtypeStruct((m, n), BF16),
        scratch_shapes=[pltpu.VMEM((tm, d), BF16)],
        compiler_params=pltpu.CompilerParams(
            dimension_semantics=("arbitrary", "arbitrary"), vmem_limit_bytes=VMEM_LIMIT),
        name="in_proj",
    )(x2, x2, norm_g, mod3, mod3, *tabs, w_in)


def _attn_kernel(lq1_ref, lk1_ref, lq2_ref, lk2_ref, sg_ref, q_ref, k_ref, v_ref, g_ref, o_ref,
                 vt_ref, qq_ref, s_ref, p_ref, al_ref, m_ref, acc_ref, *, qb, kb, dq, row_chunk, lam_init):
    seq, dv = v_ref.shape
    nq = seq // qb

    vt_ref[0:dv, :] = v_ref[...].astype(F32).T.astype(BF16)
    vt_ref[dv:, :] = jnp.ones((vt_ref.shape[0] - dv, seq), BF16)

    lam = (jnp.exp(jnp.sum(lq1_ref[...] * lk1_ref[...], axis=-1, keepdims=True))
           - jnp.exp(jnp.sum(lq2_ref[...] * lk2_ref[...], axis=-1, keepdims=True))
           + lam_init)

    def build_qq(qi):
        q = q_ref[pl.ds(_aligned(qi * qb, qb), qb), :]
        lane = lax.broadcasted_iota(jnp.int32, q.shape, 1)
        zero = jnp.zeros_like(q)
        qq_ref[0:qb, :] = jnp.where(lane < dq, q, zero)
        qq_ref[qb:, :] = jnp.where(lane >= dq, q, zero)

    nt = (((1,), (1,)), ((), ()))
    half = kb // 2

    def scores(j, slot, diag):
        k0 = _aligned(j * kb, kb)
        if not diag:
            s_ref[slot] = lax.dot_general(k_ref[pl.ds(k0, kb), :], qq_ref[...], nt,
                                          preferred_element_type=F32)
            return
        s_ref[slot, 0:half, :] = lax.dot_general(k_ref[pl.ds(k0, half), :], qq_ref[...], nt,
                                                 preferred_element_type=F32)
        for c in range(2):
            cols = slice(c * qb + half, (c + 1) * qb)
            s_ref[slot, half:kb, cols] = lax.dot_general(k_ref[pl.ds(k0 + half, half), :], qq_ref[cols, :], nt,
                                                         preferred_element_type=F32)

    def softmax(slot, diag):
        for c in range(2):
            cols = slice(c * qb, (c + 1) * qb)

            def chunk(r):
                q0 = half if diag and r >= half else 0
                s = s_ref[slot, r:r + row_chunk, c * qb + q0:(c + 1) * qb]
                if diag:
                    kpos = r + lax.broadcasted_iota(jnp.int32, s.shape, 0)
                    qpos = q0 + lax.broadcasted_iota(jnp.int32, s.shape, 1)
                    s = jnp.where(kpos <= qpos, s, NEG_BIG)
                return s, q0

            part = None
            for r in range(0, kb, row_chunk):
                s, q0 = chunk(r)
                t = jnp.max(s.reshape(row_chunk // SUBLANES, SUBLANES, qb - q0), axis=0)
                if part is None:
                    part = t
                elif q0 == 0:
                    part = jnp.maximum(part, t)
                else:
                    part = jnp.concatenate([part[:, :q0], jnp.maximum(part[:, q0:], t)], axis=1)
            m_old = m_ref[:, cols]
            m_new = jnp.maximum(m_old, jnp.max(part, axis=0, keepdims=True))
            al_ref[slot, :, cols] = jnp.exp2(m_old - m_new)
            m_ref[:, cols] = m_new
            for r in range(0, kb, row_chunk):
                s, q0 = chunk(r)
                p_ref[slot, r:r + row_chunk, c * qb + q0:(c + 1) * qb] = jnp.exp2(s - m_new[:, q0:]).astype(BF16)

    def pv_update(j, slot, diag):
        k0 = _aligned(j * kb, kb)
        nk = half if diag else kb
        pv = jnp.dot(vt_ref[:, pl.ds(k0, nk)], p_ref[slot, 0:nk, :], preferred_element_type=F32)
        acc_ref[...] = pv if j == 0 else al_ref[slot] * acc_ref[...] + pv
        if not diag:
            return
        for c in range(2):
            cols = slice(c * qb + half, (c + 1) * qb)
            acc_ref[:, cols] += jnp.dot(vt_ref[:, pl.ds(k0 + half, half)], p_ref[slot, half:kb, cols],
                                        preferred_element_type=F32)

    def finalize(qi):
        rows = pl.ds(_aligned(qi * qb, qb), qb)
        o_all = acc_ref[0:dv, :] * (1.0 / acc_ref[dv:dv + 1, :])
        o = o_all[:, :qb] - lam * o_all[:, qb:]
        ms = jnp.mean(o * o, axis=0, keepdims=True)
        on = (o * lax.rsqrt(ms + EPS)).T
        on = on * sg_ref[...] * (1.0 - lam_init)
        o_ref[rows, :] = (on * _silu(g_ref[rows, :].astype(F32))).astype(o_ref.dtype)

    pairs = [(qi, j) for qi in range(nq) for j in range(qi + 1)]
    build_qq(0)
    m_ref[...] = jnp.full(m_ref.shape, NEG_BIG, F32)
    scores(0, 0, diag=True)
    for n, (qi, j) in enumerate(pairs):
        slot, prev, nxt = n % N_SLOTS, (n - 1) % N_SLOTS, (n + 1) % N_SLOTS
        if j == qi and qi + 1 < nq:
            build_qq(qi + 1)
        is_diag = lambda pair: pair[0] == pair[1]
        softmax(slot, diag=is_diag(pairs[n]))
        if n > 0:
            pv_update(pairs[n - 1][1], prev, diag=is_diag(pairs[n - 1]))
        if n + 1 < len(pairs):
            scores(pairs[n + 1][1], nxt, diag=is_diag(pairs[n + 1]))
        if j == 0 and qi > 0:
            finalize(qi - 1)
        if j == qi:
            m_ref[...] = jnp.full(m_ref.shape, NEG_BIG, F32)
    pv_update(pairs[-1][1], (len(pairs) - 1) % N_SLOTS, diag=True)
    finalize(nq - 1)


def _diff_attn(z, lam_vecs, subln_g, *, batch, seq, dq, dv, lam_init, qb=512):
    kb = qb
    h = N_HEADS
    k_col0 = h * 2 * dq // LANES
    v_col0 = 2 * k_col0
    g_col0 = v_col0 + h * dv // LANES
    kern = functools.partial(_attn_kernel, qb=qb, kb=kb, dq=dq, row_chunk=32, lam_init=lam_init)
    return pl.pallas_call(
        kern,
        grid=(batch, h),
        in_specs=[
            *[pl.BlockSpec((1, dq), lambda b, hh: (0, 0))] * 4,
            pl.BlockSpec((1, dv), lambda b, hh: (0, 0)),
            pl.BlockSpec((seq, 2 * dq), lambda b, hh: (b, hh)),
            pl.BlockSpec((seq, 2 * dq), lambda b, hh: (b, k_col0 + hh)),
            pl.BlockSpec((seq, dv), lambda b, hh: (b, v_col0 + hh)),
            pl.BlockSpec((seq, dv), lambda b, hh: (b, g_col0 + hh)),
        ],
        out_specs=pl.BlockSpec((seq, dv), lambda b, hh: (b, hh)),
        out_shape=jax.ShapeDtypeStruct((batch * seq, h * dv), BF16),
        scratch_shapes=[pltpu.VMEM((dv + ONES_ROWS, seq), BF16),
                        pltpu.VMEM((2 * qb, 2 * dq), BF16),
                        pltpu.VMEM((N_SLOTS, kb, 2 * qb), F32),
                        pltpu.VMEM((N_SLOTS, kb, 2 * qb), BF16),
                        pltpu.VMEM((N_SLOTS, 1, 2 * qb), F32),
                        pltpu.VMEM((1, 2 * qb), F32),
                        pltpu.VMEM((dv + ONES_ROWS, 2 * qb), F32)],
        compiler_params=pltpu.CompilerParams(
            dimension_semantics=("arbitrary", "arbitrary"),
            vmem_limit_bytes=VMEM_LIMIT),
        name="diff_attn",
    )(*lam_vecs, subln_g, z, z, z, z)


def _conv_kernel(ua_ref, ub_ref, ha_ref, hb_ref, gc_ref, dww_ref, dwb_ref, lng_ref, lnb_ref,
                 wpw_ref, bpw_ref, o_ref, sh_ref, wb_ref, conv_ref, *, ts, row_chunk, lane_chunk):
    i = pl.program_id(1)
    c = ua_ref.shape[1]

    @pl.when((pl.program_id(0) == 0) & (i == 0))
    def _():
        for j in range(CONV_WIDTH):
            wb_ref[j] = jnp.broadcast_to(dww_ref[j:j + 1, :], (SUBLANES, c))

    def glu(a_ref, b_ref):
        return a_ref[...].astype(F32) * jax.nn.sigmoid(b_ref[...].astype(F32))

    hist = jnp.where(i > 0, glu(ha_ref, hb_ref), 0.0)
    ypad = jnp.concatenate([hist, glu(ua_ref, ub_ref)], axis=0)
    for r in range(SUBLANES):
        n_r = HALO + ts - (SUBLANES if r else 0)
        sh_ref[r, 0:n_r, 0:c] = ypad[r:r + n_r, :]

    base = HALO - (CONV_WIDTH - 1)
    n_rc = ts // row_chunk
    for lc in range(c // lane_chunk):
        l0 = lc * lane_chunk

        def body(rc, carry, l0=l0):
            r0 = pl.multiple_of(rc * row_chunk, row_chunk)
            parts = []
            for r in range(SUBLANES):
                taps = [j for j in range(CONV_WIDTH) if (base + j) % SUBLANES == r]
                a_lo, a_hi = (base + taps[0]) // SUBLANES, (base + taps[-1]) // SUBLANES
                n_win = row_chunk + (a_hi - a_lo) * SUBLANES
                win = sh_ref[r, pl.ds(r0 + a_lo * SUBLANES, n_win), l0:l0 + lane_chunk]
                win = win.reshape(n_win // SUBLANES, SUBLANES, lane_chunk)
                part = None
                for j in taps:
                    t0 = (base + j) // SUBLANES - a_lo
                    term = win[t0:t0 + row_chunk // SUBLANES] * wb_ref[j, :, l0:l0 + lane_chunk]
                    part = term if part is None else part + term
                parts.append(part)
                if len(parts) == 3:
                    parts = [parts[0] + (parts[1] + parts[2])]
            acc = parts[0] if len(parts) == 1 else parts[0] + parts[1]
            conv_ref[pl.ds(r0, row_chunk), l0:l0 + lane_chunk] = acc.reshape(row_chunk, lane_chunk)
            return carry

        lax.fori_loop(0, n_rc, body, 0)

    y = conv_ref[...] + dwb_ref[...]
    mu = jnp.mean(y, axis=-1, keepdims=True)
    var = jnp.mean(jnp.square(y - mu), axis=-1, keepdims=True)
    y = (y - mu) * lax.rsqrt(var + EPS) * lng_ref[...] + lnb_ref[...]
    y = _silu(y)
    y = jnp.dot(y.astype(BF16), wpw_ref[...].astype(BF16), preferred_element_type=F32) + bpw_ref[...]
    o_ref[...] = (y * _silu(gc_ref[...].astype(F32))).astype(o_ref.dtype)


def _conv_mod(z, dw_w, dw_b, ln_g, ln_b, w_pw, b_pw, *, batch, seq, c, ua_col0, ts=512):
    nt = seq // ts
    ua_blk, ub_blk, gc_blk = ua_col0 // c, ua_col0 // c + 1, ua_col0 // c + 2
    halo_per_ts = ts // HALO
    kern = functools.partial(_conv_kernel, ts=ts, row_chunk=64, lane_chunk=128)

    def halo_map(blk):
        return lambda b, i: (jnp.maximum((b * nt + i) * halo_per_ts - 1, 0), blk)

    row = lambda b, i: (0, 0)
    return pl.pallas_call(
        kern,
        grid=(batch, nt),
        in_specs=[
            pl.BlockSpec((ts, c), lambda b, i: (b * nt + i, ua_blk)),
            pl.BlockSpec((ts, c), lambda b, i: (b * nt + i, ub_blk)),
            pl.BlockSpec((HALO, c), halo_map(ua_blk)),
            pl.BlockSpec((HALO, c), halo_map(ub_blk)),
            pl.BlockSpec((ts, c), lambda b, i: (b * nt + i, gc_blk)),
            pl.BlockSpec((CONV_WIDTH, c), row),
            pl.BlockSpec((1, c), row),
            pl.BlockSpec((1, c), row),
            pl.BlockSpec((1, c), row),
            pl.BlockSpec((c, c), row),
            pl.BlockSpec((1, c), row),
        ],
        out_specs=pl.BlockSpec((ts, c), lambda b, i: (b * nt + i, 0)),
        out_shape=jax.ShapeDtypeStruct((batch * seq, c), BF16),
        scratch_shapes=[pltpu.VMEM((SUBLANES, HALO + ts, c + LANES), F32),
                        pltpu.VMEM((CONV_WIDTH, SUBLANES, c), F32),
                        pltpu.VMEM((ts, c), F32)],
        compiler_params=pltpu.CompilerParams(
            dimension_semantics=("arbitrary", "arbitrary"), vmem_limit_bytes=VMEM_LIMIT),
        name="conv_mod",
    )(z, z, z, z, z, dw_w, dw_b, ln_g, ln_b, w_pw, b_pw)


def _outproj_kernel(ya_ref, yc_ref, wa_ref, wc_ref, x_ref, gate_ref, fg_ref, o_ref, *, final_norm):
    mixed = jnp.dot(ya_ref[...], wa_ref[...].astype(BF16), preferred_element_type=F32)
    mixed = mixed + jnp.dot(yc_ref[...], wc_ref[...].astype(BF16), preferred_element_type=F32)
    xn = x_ref[...] + gate_ref[0] * mixed
    if final_norm:
        ms = jnp.mean(xn * xn, axis=-1, keepdims=True)
        xn = xn * lax.rsqrt(ms + EPS) * fg_ref[...]
    o_ref[...] = xn


def _out_proj(y_attn, y_conv, w_out, x2, mod3, final_g, *, seq, final_norm, tm=512):
    m, d = x2.shape
    da, dc = y_attn.shape[1], y_conv.shape[1]
    blocks_per_seq = seq // tm
    return pl.pallas_call(
        functools.partial(_outproj_kernel, final_norm=final_norm),
        grid=(m // tm,),
        in_specs=[
            pl.BlockSpec((tm, da), lambda i: (i, 0)),
            pl.BlockSpec((tm, dc), lambda i: (i, 0)),
            pl.BlockSpec((da, d), lambda i: (0, 0)),
            pl.BlockSpec((dc, d), lambda i: (da // dc, 0)),
            pl.BlockSpec((tm, d), lambda i: (i, 0)),
            pl.BlockSpec((1, 1, d), lambda i: ((i // blocks_per_seq) * 3 + 2, 0, 0)),
            pl.BlockSpec((1, d), lambda i: (0, 0)),
        ],
        out_specs=pl.BlockSpec((tm, d), lambda i: (i, 0)),
        out_shape=jax.ShapeDtypeStruct((m, d), F32),
        compiler_params=pltpu.CompilerParams(
            dimension_semantics=("arbitrary",), vmem_limit_bytes=VMEM_LIMIT),
        name="out_proj",
    )(y_attn, y_conv, w_out, w_out, x2, mod3, final_g)


def kernel(x, c, positions, norm_g, w_ada, b_ada, w_in, lambda_q1, lambda_k1, lambda_q2, lambda_k2,
           subln_g, conv_dw_w, conv_dw_b, conv_ln_g, conv_ln_b, w_pw, b_pw, w_out, final_g):
    batch, seq, d = x.shape
    depth = w_in.shape[0]
    d_attn = d // 2
    d_conv = d - d_attn
    dv = d_attn // N_HEADS
    dq = dv // 2
    rot = dq // 4

    tabs = _rope_tables(positions, dq=dq, rot=rot)
    c_pad = jnp.zeros((SUBLANES, d), F32).at[:batch].set(c)

    x2 = x.reshape(batch * seq, d)
    for l in range(depth):
        lam_init = 0.8 - 0.6 * math.exp(-0.3 * l)
        mod = _ada_mod(c_pad, w_ada[l], b_ada[l].reshape(1, -1))
        mod3 = mod.reshape(-1, 1, d)
        z = _in_proj(x2, norm_g[l].reshape(1, d), mod3, tabs, w_in[l],
                     seq=seq, dq=dq, rot=rot, d_qk=2 * d_attn)
        lam_vecs = [v[l].reshape(1, dq) for v in (lambda_q1, lambda_k1, lambda_q2, lambda_k2)]
        y_attn = _diff_attn(z, lam_vecs, subln_g[l].reshape(1, dv),
                            batch=batch, seq=seq, dq=dq, dv=dv, lam_init=lam_init)
        y_conv = _conv_mod(z, conv_dw_w[l], conv_dw_b[l].reshape(1, -1), conv_ln_g[l].reshape(1, -1),
                           conv_ln_b[l].reshape(1, -1), w_pw[l], b_pw[l].reshape(1, -1),
                           batch=batch, seq=seq, c=d_conv, ua_col0=4 * d_attn)
        x2 = _out_proj(y_attn, y_conv, w_out[l], x2, mod3, final_g.reshape(1, d),
                       seq=seq, final_norm=(l == depth - 1))
    return x2.reshape(batch, seq, d)
```

```python
import functools
import math

import jax
import jax.numpy as jnp
from jax import lax
from jax.experimental import pallas as pl
from jax.experimental.pallas import tpu as pltpu

F32 = jnp.float32
BF16 = jnp.bfloat16

N_HEADS = 8
CONV_WIDTH = 31
ROPE_THETA = 500000.0
EPS = 1e-6
LANES = 128
SUBLANES = 8
HALO = 32
NEG_BIG = -1e30
ONES_ROWS = 16
LOG2E = math.log2(math.e)
N_SLOTS = 3
VMEM_LIMIT = 56 * 1024 * 1024


def _aligned(start, multiple):
    return start if isinstance(start, int) else pl.multiple_of(start, multiple)


def _silu(t):
    return t * jax.nn.sigmoid(t)


def _ada_kernel(c_ref, w_hbm, b_ref, pos_ref, freq_ref, o_ref, cos_ref, sin_ref, buf_ref, sem_ref, *, tn, n_buf):
    n_chunks = o_ref.shape[1] // tn

    def chunk_copy(k):
        return pltpu.make_async_copy(w_hbm.at[:, pl.ds(k * tn, tn)], buf_ref.at[k % n_buf], sem_ref.at[k % n_buf])

    for k in range(min(n_buf, n_chunks)):
        chunk_copy(k).start()
    ang = pos_ref[...].astype(F32) * freq_ref[...]
    cos_ref[...] = jnp.cos(ang)
    sin_ref[...] = jnp.sin(ang)
    ca = _silu(c_ref[...]).astype(BF16)
    for k in range(n_chunks):
        cols = slice(k * tn, (k + 1) * tn)
        chunk_copy(k).wait()
        o_ref[:, cols] = jnp.dot(ca, buf_ref[k % n_buf].astype(BF16), preferred_element_type=F32) + b_ref[:, cols]
        if k + n_buf < n_chunks:
            chunk_copy(k + n_buf).start()


def _ada_mod(c_pad, w_ada, b_ada, pos_rep, freq_lane, tn=768, n_buf=3):
    rows, d = c_pad.shape
    n = w_ada.shape[1]
    vmem = pl.BlockSpec(memory_space=pltpu.VMEM)
    return pl.pallas_call(
        functools.partial(_ada_kernel, tn=tn, n_buf=n_buf),
        in_specs=[vmem, pl.BlockSpec(memory_space=pl.ANY), vmem, vmem, vmem],
        out_specs=[vmem, vmem, vmem],
        out_shape=[jax.ShapeDtypeStruct((rows, n), F32)] + [jax.ShapeDtypeStruct(pos_rep.shape, F32)] * 2,
        scratch_shapes=[pltpu.VMEM((n_buf, d, tn), F32),
                        pltpu.SemaphoreType.DMA((n_buf,))],
        compiler_params=pltpu.CompilerParams(vmem_limit_bytes=VMEM_LIMIT),
        name="ada_mod",
    )(c_pad, w_ada, b_ada, pos_rep, freq_lane)


def _rope_angles(positions, *, rot):
    n = positions.size
    nf = rot // 2
    inv_freq = ROPE_THETA ** (-jnp.arange(0, rot, 2, dtype=F32) / rot)
    freq_lane = jnp.tile(inv_freq, LANES // nf).reshape(1, LANES)
    pos_rep = jnp.repeat(positions.reshape(-1), nf).reshape(n * nf // LANES, LANES)
    return pos_rep, freq_lane


def _rope_tables(cos, sin, *, n, dq, rot):
    nf = rot // 2
    cos, sin = cos.reshape(n, nf), sin.reshape(n, nf)
    one, zero = jnp.ones((n, dq - rot), F32), jnp.zeros((n, dq - rot), F32)
    tab_c = jnp.tile(jnp.concatenate([cos, cos, one], axis=1), (1, LANES // dq))
    tab_s = jnp.tile(jnp.concatenate([sin, sin, zero], axis=1), (1, LANES // dq))
    return tab_c, tab_s


def _inproj_kernel(xt_ref, xb_ref, g_ref, scale_ref, shift_ref, tc_ref, ts_ref, w_ref, o_ref,
                   h_ref, *, dq, rot, q_scale, n_rope_blocks, chunk, norm_rows_per_iter):
    j = pl.program_id(1)

    @pl.when(j == 0)
    def _():
        gs = g_ref[...] * (1.0 + scale_ref[0])
        sh = shift_ref[0]
        half = xt_ref.shape[0]
        for part, x_ref in enumerate((xt_ref, xb_ref)):

            def norm_rows(r, carry, part=part, x_ref=x_ref):
                r0 = pl.multiple_of(r * norm_rows_per_iter, norm_rows_per_iter)
                x = x_ref[pl.ds(r0, norm_rows_per_iter), :]
                ms = jnp.mean(x * x, axis=-1, keepdims=True)
                h_ref[pl.ds(part * half + r0, norm_rows_per_iter), :] = (
                    x * lax.rsqrt(ms + EPS) * gs + sh).astype(BF16)
                return carry

            lax.fori_loop(0, half // norm_rows_per_iter, norm_rows, 0, unroll=8)

    @pl.when(j < n_rope_blocks)
    def _():
        sc = jnp.where(j < n_rope_blocks // 2, q_scale, 1.0).astype(F32)
        cs, sn = tc_ref[...] * sc, ts_ref[...] * sc
        d = lax.broadcasted_iota(jnp.int32, sn.shape, 1) % dq
        su = jnp.where(d < rot // 2, sn, 0.0)
        sd = jnp.where(d >= rot // 2, sn, 0.0)
        for c in range(w_ref.shape[1] // chunk):
            acc = jnp.dot(h_ref[...], w_ref[:, c * chunk:(c + 1) * chunk].astype(BF16),
                          preferred_element_type=F32)
            for t in range(chunk // LANES):
                a = acc[:, t * LANES:(t + 1) * LANES]
                up = pltpu.roll(a, LANES - rot // 2, 1)
                dn = pltpu.roll(a, rot // 2, 1)
                col = c * chunk + t * LANES
                o_ref[:, col:col + LANES] = (a * cs - up * su + dn * sd).astype(o_ref.dtype)

    @pl.when(j >= n_rope_blocks)
    def _():
        o_ref[...] = jnp.dot(h_ref[...], w_ref[...].astype(BF16),
                             preferred_element_type=F32).astype(o_ref.dtype)


def _in_proj(x2, norm_g, mod3, tabs, w_in, *, seq, dq, rot, d_qk, tm=1024, tn=1024):
    m, d = x2.shape
    n = w_in.shape[1]
    blocks_per_seq = seq // tm
    kern = functools.partial(_inproj_kernel, dq=dq, rot=rot, q_scale=LOG2E / math.sqrt(dq),
                             n_rope_blocks=d_qk // tn, chunk=256, norm_rows_per_iter=16)
    tab_spec = pl.BlockSpec((tm, LANES), lambda i, j: (i, 0))
    n_rows, n_cols = m // tm, n // tn

    def x_half(part):
        switch = n_cols - 2 + part
        return pl.BlockSpec((tm // 2, d), lambda i, j: (
            2 * jnp.where(j >= switch, jnp.minimum(i + 1, n_rows - 1), i) + part, 0))

    return pl.pallas_call(
        kern,
        grid=(n_rows, n_cols),
        in_specs=[
            x_half(0), x_half(1),
            pl.BlockSpec((1, d), lambda i, j: (0, 0)),
            pl.BlockSpec((1, 1, d), lambda i, j: ((i // blocks_per_seq) * 3 + 1, 0, 0)),
            pl.BlockSpec((1, 1, d), lambda i, j: ((i // blocks_per_seq) * 3 + 0, 0, 0)),
            tab_spec, tab_spec,
            pl.BlockSpec((d, tn), lambda i, j: (0, j)),
        ],
        out_specs=pl.BlockSpec((tm, tn), lambda i, j: (i, j)),
        out_shape=jax.ShapeDtypeStruct((m, n), BF16),
        scratch_shapes=[pltpu.VMEM((tm, d), BF16)],
        compiler_params=pltpu.CompilerParams(
            dimension_semantics=("arbitrary", "arbitrary"), vmem_limit_bytes=VMEM_LIMIT),
        name="in_proj",
    )(x2, x2, norm_g, mod3, mod3, *tabs, w_in)


def _attn_kernel(lq1_ref, lk1_ref, lq2_ref, lk2_ref, sg_ref, q_ref, k_ref, v_ref, g_ref, o_ref,
                 vt_ref, qq_ref, s_ref, p_ref, al_ref, m_ref, acc_ref, *, qb, kb, dq, row_chunk, lam_init):
    seq, dv = v_ref.shape
    nq = seq // qb

    vt_ref[0:dv, :] = v_ref[...].astype(F32).T.astype(BF16)
    vt_ref[dv:, :] = jnp.ones((vt_ref.shape[0] - dv, seq), BF16)

    lam = (jnp.exp(jnp.sum(lq1_ref[...] * lk1_ref[...], axis=-1, keepdims=True))
           - jnp.exp(jnp.sum(lq2_ref[...] * lk2_ref[...], axis=-1, keepdims=True))
           + lam_init)

    def build_qq(qi):
        q = q_ref[pl.ds(_aligned(qi * qb, qb), qb), :]
        lane = lax.broadcasted_iota(jnp.int32, q.shape, 1)
        zero = jnp.zeros_like(q)
        qq_ref[0:qb, :] = jnp.where(lane < dq, q, zero)
        qq_ref[qb:, :] = jnp.where(lane >= dq, q, zero)

    nt = (((1,), (1,)), ((), ()))
    half = kb // 2

    def scores(j, slot, diag):
        k0 = _aligned(j * kb, kb)
        if not diag:
            s_ref[slot] = lax.dot_general(k_ref[pl.ds(k0, kb), :], qq_ref[...], nt,
                                          preferred_element_type=F32)
            return
        s_ref[slot, 0:half, :] = lax.dot_general(k_ref[pl.ds(k0, half), :], qq_ref[...], nt,
                                                 preferred_element_type=F32)
        for c in range(2):
            cols = slice(c * qb + half, (c + 1) * qb)
            s_ref[slot, half:kb, cols] = lax.dot_general(k_ref[pl.ds(k0 + half, half), :], qq_ref[cols, :], nt,
                                                         preferred_element_type=F32)

    def softmax(slot, diag):
        for c in range(2):
            cols = slice(c * qb, (c + 1) * qb)

            def chunk(r):
                q0 = half if diag and r >= half else 0
                s = s_ref[slot, r:r + row_chunk, c * qb + q0:(c + 1) * qb]
                if diag:
                    kpos = r + lax.broadcasted_iota(jnp.int32, s.shape, 0)
                    qpos = q0 + lax.broadcasted_iota(jnp.int32, s.shape, 1)
                    s = jnp.where(kpos <= qpos, s, NEG_BIG)
                return s, q0

            part = None
            for r in range(0, kb, row_chunk):
                s, q0 = chunk(r)
                t = jnp.max(s.reshape(row_chunk // SUBLANES, SUBLANES, qb - q0), axis=0)
                if part is None:
                    part = t
                elif q0 == 0:
                    part = jnp.maximum(part, t)
                else:
                    part = jnp.concatenate([part[:, :q0], jnp.maximum(part[:, q0:], t)], axis=1)
            m_old = m_ref[:, cols]
            m_new = jnp.maximum(m_old, jnp.max(part, axis=0, keepdims=True))
            al_ref[slot, :, cols] = jnp.exp2(m_old - m_new)
            m_ref[:, cols] = m_new
            for r in range(0, kb, row_chunk):
                s, q0 = chunk(r)
                p_ref[slot, r:r + row_chunk, c * qb + q0:(c + 1) * qb] = jnp.exp2(s - m_new[:, q0:]).astype(BF16)

    def pv_update(j, slot, diag):
        k0 = _aligned(j * kb, kb)
        nk = half if diag else kb
        pv = jnp.dot(vt_ref[:, pl.ds(k0, nk)], p_ref[slot, 0:nk, :], preferred_element_type=F32)
        acc_ref[...] = pv if j == 0 else al_ref[slot] * acc_ref[...] + pv
        if not diag:
            return
        for c in range(2):
            cols = slice(c * qb + half, (c + 1) * qb)
            acc_ref[:, cols] += jnp.dot(vt_ref[:, pl.ds(k0 + half, half)], p_ref[slot, half:kb, cols],
                                        preferred_element_type=F32)

    def finalize(qi):
        rows = pl.ds(_aligned(qi * qb, qb), qb)
        o_all = acc_ref[0:dv, :] * (1.0 / acc_ref[dv:dv + 1, :])
        o = o_all[:, :qb] - lam * o_all[:, qb:]
        ms = jnp.mean(o * o, axis=0, keepdims=True)
        on = (o * lax.rsqrt(ms + EPS)).T
        on = on * sg_ref[...] * (1.0 - lam_init)
        o_ref[rows, :] = (on * _silu(g_ref[rows, :].astype(F32))).astype(o_ref.dtype)

    pairs = [(qi, j) for qi in range(nq) for j in range(qi + 1)]
    build_qq(0)
    m_ref[...] = jnp.full(m_ref.shape, NEG_BIG, F32)
    scores(0, 0, diag=True)
    for n, (qi, j) in enumerate(pairs):
        slot, prev, nxt = n % N_SLOTS, (n - 1) % N_SLOTS, (n + 1) % N_SLOTS
        if j == qi and qi + 1 < nq:
            build_qq(qi + 1)
        is_diag = lambda pair: pair[0] == pair[1]
        softmax(slot, diag=is_diag(pairs[n]))
        if n > 0:
            pv_update(pairs[n - 1][1], prev, diag=is_diag(pairs[n - 1]))
        if n + 1 < len(pairs):
            scores(pairs[n + 1][1], nxt, diag=is_diag(pairs[n + 1]))
        if j == 0 and qi > 0:
            finalize(qi - 1)
        if j == qi:
            m_ref[...] = jnp.full(m_ref.shape, NEG_BIG, F32)
    pv_update(pairs[-1][1], (len(pairs) - 1) % N_SLOTS, diag=True)
    finalize(nq - 1)


def _diff_attn(z, lam_vecs, subln_g, *, batch, seq, dq, dv, lam_init, qb=512):
    kb = qb
    h = N_HEADS
    k_col0 = h * 2 * dq // LANES
    v_col0 = 2 * k_col0
    g_col0 = v_col0 + h * dv // LANES
    kern = functools.partial(_attn_kernel, qb=qb, kb=kb, dq=dq, row_chunk=32, lam_init=lam_init)
    return pl.pallas_call(
        kern,
        grid=(batch, h),
        in_specs=[
            *[pl.BlockSpec((1, dq), lambda b, hh: (0, 0))] * 4,
            pl.BlockSpec((1, dv), lambda b, hh: (0, 0)),
            pl.BlockSpec((seq, 2 * dq), lambda b, hh: (b, hh)),
            pl.BlockSpec((seq, 2 * dq), lambda b, hh: (b, k_col0 + hh)),
            pl.BlockSpec((seq, dv), lambda b, hh: (b, v_col0 + hh)),
            pl.BlockSpec((seq, dv), lambda b, hh: (b, g_col0 + hh)),
        ],
        out_specs=pl.BlockSpec((seq, dv), lambda b, hh: (b, hh)),
        out_shape=jax.ShapeDtypeStruct((batch * seq, h * dv), BF16),
        scratch_shapes=[pltpu.VMEM((dv + ONES_ROWS, seq), BF16),
                        pltpu.VMEM((2 * qb, 2 * dq), BF16),
                        pltpu.VMEM((N_SLOTS, kb, 2 * qb), F32),
                        pltpu.VMEM((N_SLOTS, kb, 2 * qb), BF16),
                        pltpu.VMEM((N_SLOTS, 1, 2 * qb), F32),
                        pltpu.VMEM((1, 2 * qb), F32),
                        pltpu.VMEM((dv + ONES_ROWS, 2 * qb), F32)],
        compiler_params=pltpu.CompilerParams(
            dimension_semantics=("arbitrary", "arbitrary"),
            vmem_limit_bytes=VMEM_LIMIT),
        name="diff_attn",
    )(*lam_vecs, subln_g, z, z, z, z)


def _conv_kernel(ua_ref, ub_ref, ha_ref, hb_ref, gc_ref, dww_ref, dwb_ref, lng_ref, lnb_ref,
                 wpw_ref, bpw_ref, o_ref, sh_ref, wb_ref, conv_ref, *, ts, row_chunk, lane_chunk):
    i = pl.program_id(1)
    c = ua_ref.shape[1]

    @pl.when((pl.program_id(0) == 0) & (i == 0))
    def _():
        for j in range(CONV_WIDTH):
            wb_ref[j] = jnp.broadcast_to(dww_ref[j:j + 1, :], (SUBLANES, c))

    def glu(a_ref, b_ref):
        return a_ref[...].astype(F32) * jax.nn.sigmoid(b_ref[...].astype(F32))

    hist = jnp.where(i > 0, glu(ha_ref, hb_ref), 0.0)
    ypad = jnp.concatenate([hist, glu(ua_ref, ub_ref)], axis=0)
    for r in range(SUBLANES):
        n_r = HALO + ts - (SUBLANES if r else 0)
        sh_ref[r, 0:n_r, 0:c] = ypad[r:r + n_r, :]

    base = HALO - (CONV_WIDTH - 1)
    n_rc = ts // row_chunk
    for lc in range(c // lane_chunk):
        l0 = lc * lane_chunk

        def body(rc, carry, l0=l0):
            r0 = pl.multiple_of(rc * row_chunk, row_chunk)
            parts = []
            for r in range(SUBLANES):
                taps = [j for j in range(CONV_WIDTH) if (base + j) % SUBLANES == r]
                a_lo, a_hi = (base + taps[0]) // SUBLANES, (base + taps[-1]) // SUBLANES
                n_win = row_chunk + (a_hi - a_lo) * SUBLANES
                win = sh_ref[r, pl.ds(r0 + a_lo * SUBLANES, n_win), l0:l0 + lane_chunk]
                win = win.reshape(n_win // SUBLANES, SUBLANES, lane_chunk)
                part = None
                for j in taps:
                    t0 = (base + j) // SUBLANES - a_lo
                    term = win[t0:t0 + row_chunk // SUBLANES] * wb_ref[j, :, l0:l0 + lane_chunk]
                    part = term if part is None else part + term
                parts.append(part)
                if len(parts) == 3:
                    parts = [parts[0] + (parts[1] + parts[2])]
            acc = parts[0] if len(parts) == 1 else parts[0] + parts[1]
            conv_ref[pl.ds(r0, row_chunk), l0:l0 + lane_chunk] = acc.reshape(row_chunk, lane_chunk)
            return carry

        lax.fori_loop(0, n_rc, body, 0)

    y = conv_ref[...] + dwb_ref[...]
    mu = jnp.mean(y, axis=-1, keepdims=True)
    var = jnp.mean(jnp.square(y - mu), axis=-1, keepdims=True)
    y = (y - mu) * lax.rsqrt(var + EPS) * lng_ref[...] + lnb_ref[...]
    y = _silu(y)
    y = jnp.dot(y.astype(BF16), wpw_ref[...].astype(BF16), preferred_element_type=F32) + bpw_ref[...]
    o_ref[...] = (y * _silu(gc_ref[...].astype(F32))).astype(o_ref.dtype)


def _conv_mod(z, dw_w, dw_b, ln_g, ln_b, w_pw, b_pw, *, batch, seq, c, ua_col0, ts=512):
    nt = seq // ts
    ua_blk, ub_blk, gc_blk = ua_col0 // c, ua_col0 // c + 1, ua_col0 // c + 2
    halo_per_ts = ts // HALO
    kern = functools.partial(_conv_kernel, ts=ts, row_chunk=64, lane_chunk=128)

    def halo_map(blk):
        return lambda b, i: (jnp.maximum((b * nt + i) * halo_per_ts - 1, 0), blk)

    row = lambda b, i: (0, 0)
    return pl.pallas_call(
        kern,
        grid=(batch, nt),
        in_specs=[
            pl.BlockSpec((ts, c), lambda b, i: (b * nt + i, ua_blk)),
            pl.BlockSpec((ts, c), lambda b, i: (b * nt + i, ub_blk)),
            pl.BlockSpec((HALO, c), halo_map(ua_blk)),
            pl.BlockSpec((HALO, c), halo_map(ub_blk)),
            pl.BlockSpec((ts, c), lambda b, i: (b * nt + i, gc_blk)),
            pl.BlockSpec((CONV_WIDTH, c), row),
            pl.BlockSpec((1, c), row),
            pl.BlockSpec((1, c), row),
            pl.BlockSpec((1, c), row),
            pl.BlockSpec((c, c), row),
            pl.BlockSpec((1, c), row),
        ],
        out_specs=pl.BlockSpec((ts, c), lambda b, i: (b * nt + i, 0)),
        out_shape=jax.ShapeDtypeStruct((batch * seq, c), BF16),
        scratch_shapes=[pltpu.VMEM((SUBLANES, HALO + ts, c + LANES), F32),
                        pltpu.VMEM((CONV_WIDTH, SUBLANES, c), F32),
                        pltpu.VMEM((ts, c), F32)],
        compiler_params=pltpu.CompilerParams(
            dimension_semantics=("arbitrary", "arbitrary"), vmem_limit_bytes=VMEM_LIMIT),
        name="conv_mod",
    )(z, z, z, z, z, dw_w, dw_b, ln_g, ln_b, w_pw, b_pw)


def _outproj_kernel(ya_ref, yc_ref, wa_ref, wc_ref, x_ref, gate_ref, fg_ref, o_ref, *, final_norm):
    mixed = jnp.dot(ya_ref[...], wa_ref[...].astype(BF16), preferred_element_type=F32)
    mixed = mixed + jnp.dot(yc_ref[...], wc_ref[...].astype(BF16), preferred_element_type=F32)
    xn = x_ref[...] + gate_ref[0] * mixed
    if final_norm:
        ms = jnp.mean(xn * xn, axis=-1, keepdims=True)
        xn = xn * lax.rsqrt(ms + EPS) * fg_ref[...]
    o_ref[...] = xn


def _out_proj(y_attn, y_conv, w_out, x2, mod3, final_g, *, seq, final_norm, tm=512):
    m, d = x2.shape
    da, dc = y_attn.shape[1], y_conv.shape[1]
    blocks_per_seq = seq // tm
    return pl.pallas_call(
        functools.partial(_outproj_kernel, final_norm=final_norm),
        grid=(m // tm,),
        in_specs=[
            pl.BlockSpec((tm, da), lambda i: (i, 0)),
            pl.BlockSpec((tm, dc), lambda i: (i, 0)),
            pl.BlockSpec((da, d), lambda i: (0, 0)),
            pl.BlockSpec((dc, d), lambda i: (da // dc, 0)),
            pl.BlockSpec((tm, d), lambda i: (i, 0)),
            pl.BlockSpec((1, 1, d), lambda i: ((i // blocks_per_seq) * 3 + 2, 0, 0)),
            pl.BlockSpec((1, d), lambda i: (0, 0)),
        ],
        out_specs=pl.BlockSpec((tm, d), lambda i: (i, 0)),
        out_shape=jax.ShapeDtypeStruct((m, d), F32),
        compiler_params=pltpu.CompilerParams(
            dimension_semantics=("arbitrary",), vmem_limit_bytes=VMEM_LIMIT),
        name="out_proj",
    )(y_attn, y_conv, w_out, w_out, x2, mod3, final_g)


def kernel(x, c, positions, norm_g, w_ada, b_ada, w_in, lambda_q1, lambda_k1, lambda_q2, lambda_k2,
           subln_g, conv_dw_w, conv_dw_b, conv_ln_g, conv_ln_b, w_pw, b_pw, w_out, final_g):
    batch, seq, d = x.shape
    depth = w_in.shape[0]
    d_attn = d // 2
    d_conv = d - d_attn
    dv = d_attn // N_HEADS
    dq = dv // 2
    rot = dq // 4

    pos_rep, freq_lane = _rope_angles(positions, rot=rot)
    c_pad = jnp.zeros((SUBLANES, d), F32).at[:batch].set(c)

    x2 = x.reshape(batch * seq, d)
    for l in range(depth):
        lam_init = 0.8 - 0.6 * math.exp(-0.3 * l)
        mod, cos, sin = _ada_mod(c_pad, w_ada[l], b_ada[l].reshape(1, -1), pos_rep, freq_lane)
        tabs = _rope_tables(cos, sin, n=batch * seq, dq=dq, rot=rot)
        mod3 = mod.reshape(-1, 1, d)
        z = _in_proj(x2, norm_g[l].reshape(1, d), mod3, tabs, w_in[l],
                     seq=seq, dq=dq, rot=rot, d_qk=2 * d_attn)
        lam_vecs = [v[l].reshape(1, dq) for v in (lambda_q1, lambda_k1, lambda_q2, lambda_k2)]
        y_attn = _diff_attn(z, lam_vecs, subln_g[l].reshape(1, dv),
                            batch=batch, seq=seq, dq=dq, dv=dv, lam_init=lam_init)
        y_conv = _conv_mod(z, conv_dw_w[l], conv_dw_b[l].reshape(1, -1), conv_ln_g[l].reshape(1, -1),
                           conv_ln_b[l].reshape(1, -1), w_pw[l], b_pw[l].reshape(1, -1),
                           batch=batch, seq=seq, c=d_conv, ua_col0=4 * d_attn)
        x2 = _out_proj(y_attn, y_conv, w_out[l], x2, mod3, final_g.reshape(1, d),
                       seq=seq, final_norm=(l == depth - 1))
    return x2.reshape(batch, seq, d)
```
